```python
import math
import jax, jax.numpy as jnp
from jax import lax
import numpy as np

D_MODEL = 2048
BATCH = 8
SEQ = 4096
DEPTH = 4

A_WINDOWS = (128, 512, 2048)
A_DILATIONS = (1, 4, 16)
A_GROUPS = len(A_WINDOWS)
A_HEADS = 8
HEAD_DIM = 128
A_WIDTH = A_HEADS * HEAD_DIM
ATTN_BLOCK = 128
B_WIDTH = D_MODEL - A_WIDTH
B_CONV = 3
C_WIDTH = D_MODEL // 2
C_CONV = 31
D_WIDTH = D_MODEL - C_WIDTH
D_WINDOWS = (2, 4, 8, 16)
D_GROUP = D_WIDTH // len(D_WINDOWS)
D_FF = ((8 * D_MODEL // 3 + 255) // 256) * 256
EPS = 1e-6
N_EVEN = (DEPTH + 1) // 2
N_ODD = DEPTH // 2
EVEN_IN = 3 * A_GROUPS * A_WIDTH + 3 * B_WIDTH
ODD_IN = 2 * C_WIDTH + D_WIDTH

kernel_name = "hybrid_dilated_attn_shortconv_conformer_pool"


def _rmsnorm(x, g):
    xf = x.astype(jnp.float32)
    y = xf * lax.rsqrt(jnp.mean(xf * xf, axis=-1, keepdims=True) + EPS)
    return (y * g.astype(jnp.float32)).astype(x.dtype)


def _layernorm(x, g, b):
    xf = x.astype(jnp.float32)
    mu = jnp.mean(xf, axis=-1, keepdims=True)
    var = jnp.mean(jnp.square(xf - mu), axis=-1, keepdims=True)
    y = (xf - mu) * lax.rsqrt(var + EPS)
    return (y * g.astype(jnp.float32) + b.astype(jnp.float32)).astype(x.dtype)


def _swiglu(xn, w1, w3, w2):
    return (jax.nn.silu(xn @ w1) * (xn @ w3)) @ w2


def _causal_dwconv(u, w):
    K, C = w.shape
    return lax.conv_general_dilated(u, w[:, None, :].astype(u.dtype), window_strides=(1,),
                                    padding=[(K - 1, 0)],
                                    dimension_numbers=('NWC', 'WIO', 'NWC'),
                                    feature_group_count=C)


def _alibi_slopes(n):
    return jnp.asarray(2.0 ** (-8.0 * np.arange(1, n + 1) / n), dtype=jnp.float32)


def _dilated_attention(q, k, v, dilation, window, slopes):
    Bn, S, H, Dh = q.shape
    d = dilation
    L = S // d
    Q = ATTN_BLOCK
    w_sub = window // d
    nblk = -(-L // Q)
    Lp = nblk * Q

    def blocks(t):
        t = t.reshape(Bn, L, d, H, Dh).transpose(0, 2, 1, 3, 4)
        t = jnp.pad(t, ((0, 0), (0, 0), (0, Lp - L), (0, 0), (0, 0)))
        return t.reshape(Bn, d, nblk, Q, H, Dh)

    def with_prev(t):
        prev = jnp.pad(t, ((0, 0), (0, 0), (1, 0), (0, 0), (0, 0), (0, 0)))[:, :, :-1]
        return jnp.concatenate([prev, t], axis=3)

    qb = blocks(q)
    kk = with_prev(blocks(k))
    vv = with_prev(blocks(v))
    s = jnp.einsum('brnqhd,brnkhd->brnhqk', qb, kk,
                   preferred_element_type=jnp.float32) * (Dh ** -0.5)
    qi = jnp.arange(Q)[:, None]
    ci = jnp.arange(2 * Q)[None, :]
    dist = qi + Q - ci
    key_pos = jnp.arange(nblk)[:, None, None] * Q - Q + ci[None]
    valid = (dist >= 0) & (dist <= w_sub) & (key_pos >= 0)
    bias = -slopes[:, None, None] * (dist * d).astype(jnp.float32)
    s = jnp.where(valid[:, None], s + bias, -jnp.inf)
    lse = jax.nn.logsumexp(s, axis=-1)
    p = jnp.exp(s - lse[..., None])
    o = jnp.einsum('brnhqk,brnkhd->brnqhd', p.astype(v.dtype), vv,
                   preferred_element_type=jnp.float32)

    def unblock(t):
        t = t.reshape((Bn, d, Lp) + t.shape[4:])[:, :, :L]
        t = jnp.swapaxes(t, 1, 2)
        return t.reshape((Bn, S) + t.shape[3:])

    return unblock(o), unblock(jnp.swapaxes(lse, 3, 4))


def _even_mixer(xn, w_in, q_gain, k_gain, conv_w, w_out):
    Bn, S, _ = xn.shape
    h = xn @ w_in
    n_qkv = 3 * A_GROUPS * A_WIDTH
    qkv = h[..., :n_qkv].reshape(Bn, S, 3, A_GROUPS, A_HEADS, HEAD_DIM)
    q = _rmsnorm(qkv[:, :, 0], q_gain)
    k = _rmsnorm(qkv[:, :, 1], k_gain)
    v = qkv[:, :, 2]
    slopes = _alibi_slopes(A_HEADS)
    outs, lses = [], []
    for g in range(A_GROUPS):
        o, l = _dilated_attention(q[:, :, g], k[:, :, g], v[:, :, g],
                                  A_DILATIONS[g], A_WINDOWS[g], slopes)
        outs.append(o)
        lses.append(l)
    alpha = jax.nn.softmax(jnp.stack(lses), axis=0)
    y_a = jnp.einsum('gbsh,gbshd->bshd', alpha, jnp.stack(outs))
    y_a = y_a.reshape(Bn, S, A_WIDTH).astype(xn.dtype)
    b_gate, c_gate, xt = jnp.split(h[..., n_qkv:], 3, axis=-1)
    y_b = b_gate * _causal_dwconv(c_gate * xt, conv_w)
    return jnp.concatenate([y_a, y_b], axis=-1) @ w_out


def _odd_mixer(xn, w_in, conv_w, conv_b, ln_g, ln_b, pool_w, pool_scale, w_out):
    Bn, S, _ = xn.shape
    h = xn @ w_in
    u = h[..., :C_WIDTH] * jax.nn.sigmoid(h[..., C_WIDTH:2 * C_WIDTH])
    u = _causal_dwconv(u, conv_w) + conv_b
    u = jax.nn.silu(_layernorm(u, ln_g, ln_b))
    z = h[..., 2 * C_WIDTH:].reshape(Bn, S, len(D_WINDOWS), D_GROUP)
    zf = z.astype(jnp.float32)
    cs = jnp.cumsum(zf, axis=1)
    t1 = jnp.arange(1, S + 1, dtype=jnp.float32)
    pooled = []
    for g, kw in enumerate(D_WINDOWS):
        c = cs[:, :, g]
        lo = jnp.pad(c, ((0, 0), (kw, 0), (0, 0)))[:, :S]
        pooled.append((c - lo) / jnp.minimum(t1, float(kw))[None, :, None])
    pooled = jnp.stack(pooled, axis=2) - zf
    y_d = jnp.einsum('bsgc,gce->bsge', pooled.astype(xn.dtype), pool_w)
    y_d = y_d.reshape(Bn, S, D_WIDTH) * pool_scale
    return jnp.concatenate([u, y_d], axis=-1) @ w_out


def setup_inputs(seed: int = 0) -> dict:
    key = jax.random.key(seed)
    ks = jax.random.split(key, 20)
    f32 = jnp.float32

    def nrm(k, shape, scale):
        return jax.random.normal(k, shape, f32) * scale

    return {
        "x": nrm(ks[0], (BATCH, SEQ, D_MODEL), 1.0),
        "norm_g": 1.0 + nrm(ks[1], (DEPTH, 3, D_MODEL), 0.02),
        "ffn_w1": nrm(ks[2], (DEPTH, 2, D_MODEL, D_FF), D_MODEL ** -0.5),
        "ffn_w3": nrm(ks[3], (DEPTH, 2, D_MODEL, D_FF), D_MODEL ** -0.5),
        "ffn_w2": nrm(ks[4], (DEPTH, 2, D_FF, D_MODEL), D_FF ** -0.5),
        "ev_w_in": nrm(ks[5], (N_EVEN, D_MODEL, EVEN_IN), D_MODEL ** -0.5),
        "ev_q_gain": 1.0 + nrm(ks[6], (N_EVEN, HEAD_DIM), 0.02),
        "ev_k_gain": 1.0 + nrm(ks[7], (N_EVEN, HEAD_DIM), 0.02),
        "ev_conv_w": nrm(ks[8], (N_EVEN, B_CONV, B_WIDTH), B_CONV ** -0.5),
        "ev_w_out": nrm(ks[9], (N_EVEN, D_MODEL, D_MODEL), D_MODEL ** -0.5),
        "od_w_in": nrm(ks[10], (N_ODD, D_MODEL, ODD_IN), D_MODEL ** -0.5),
        "od_conv_w": nrm(ks[11], (N_ODD, C_CONV, C_WIDTH), C_CONV ** -0.5),
        "od_conv_b": nrm(ks[12], (N_ODD, C_WIDTH), 0.02),
        "od_ln_g": 1.0 + nrm(ks[13], (N_ODD, C_WIDTH), 0.02),
        "od_ln_b": nrm(ks[14], (N_ODD, C_WIDTH), 0.02),
        "od_pool_w": nrm(ks[15], (N_ODD, len(D_WINDOWS), D_GROUP, D_GROUP), D_GROUP ** -0.5),
        "od_pool_scale": 1.0 + nrm(ks[16], (N_ODD, D_WIDTH), 0.02),
        "od_w_out": nrm(ks[17], (N_ODD, D_MODEL, D_MODEL), D_MODEL ** -0.5),
    }


def reference(x, norm_g, ffn_w1, ffn_w3, ffn_w2, ev_w_in, ev_q_gain, ev_k_gain, ev_conv_w,
              ev_w_out, od_w_in, od_conv_w, od_conv_b, od_ln_g, od_ln_b, od_pool_w,
              od_pool_scale, od_w_out):
    for layer in range(DEPTH):
        x = x + 0.5 * _swiglu(_rmsnorm(x, norm_g[layer, 0]),
                              ffn_w1[layer, 0], ffn_w3[layer, 0], ffn_w2[layer, 0])
        xn = _rmsnorm(x, norm_g[layer, 1])
        if layer % 2 == 0:
            i = layer // 2
            x = x + _even_mixer(xn, ev_w_in[i], ev_q_gain[i], ev_k_gain[i],
                                ev_conv_w[i], ev_w_out[i])
        else:
            i = layer // 2
            x = x + _odd_mixer(xn, od_w_in[i], od_conv_w[i], od_conv_b[i], od_ln_g[i],
                               od_ln_b[i], od_pool_w[i], od_pool_scale[i], od_w_out[i])
        x = x + 0.5 * _swiglu(_rmsnorm(x, norm_g[layer, 2]),
                              ffn_w1[layer, 1], ffn_w3[layer, 1], ffn_w2[layer, 1])
    return x
```

```python
import functools

import numpy as np
import jax
import jax.numpy as jnp
from jax import lax
from jax.experimental import pallas as pl
from jax.experimental.pallas import tpu as pltpu

F32 = jnp.float32
BF16 = jnp.bfloat16
EPS = 1e-6

HEAD_DIM = 128
N_HEADS = 8
A_WIDTH = N_HEADS * HEAD_DIM
WINDOWS = (128, 512, 2048)
DILATIONS = (1, 4, 16)
ATTN_BLOCK = 128
POOL_WINDOWS = (2, 4, 8, 16)

LANES = 128
SUBLANES = 8
VMEM_LIMIT = 60000 * 1024
FFN_TM = 1024
FFN_TF = 512
PROJ_TM = 512
PROJ_TN = 512
ATTN_TL = 512
CONV_HALO = 32
POOL_HALO = 16


def _params(semantics):
    return pltpu.CompilerParams(dimension_semantics=semantics, vmem_limit_bytes=VMEM_LIMIT)


def _resident(block_shape, index_map):
    return pl.BlockSpec(block_shape, index_map, pipeline_mode=pl.Buffered(1))


def _rms(x, gain):
    ms = jnp.mean(x * x, axis=-1, keepdims=True)
    return x * lax.rsqrt(ms + EPS) * gain


def _ffn_kernel(*refs, with_next):
    if with_next:
        x_ref, g_ref, w1_ref, w3_ref, w2_ref, gn_ref, o_ref, xn_ref, xs_ref = refs
    else:
        x_ref, g_ref, w1_ref, w3_ref, w2_ref, o_ref, xs_ref = refs
    j = pl.program_id(1)

    @pl.when(j == 0)
    def _():
        xf = x_ref[...]
        xs_ref[...] = _rms(xf, g_ref[...]).astype(BF16)
        o_ref[...] = xf

    xn = xs_ref[...]
    h1 = jnp.dot(xn, w1_ref[...], preferred_element_type=F32)
    h3 = jnp.dot(xn, w3_ref[...], preferred_element_type=F32)
    a = (0.5 * (h1 * jax.nn.sigmoid(h1)) * h3).astype(BF16)
    o_ref[...] += jnp.dot(a, w2_ref[...], preferred_element_type=F32)

    if with_next:
        @pl.when(j == pl.num_programs(1) - 1)
        def _():
            xn_ref[...] = _rms(o_ref[...], gn_ref[...]).astype(BF16)


def _ffn(x, norm_g, w1, w3, w2, layer, which, gain_idx, next_gain_idx=None):
    n, d = x.shape
    f = w1.shape[-1]
    tm, tf = FFN_TM, FFN_TF
    with_next = next_gain_idx is not None
    in_specs = [
        pl.BlockSpec((tm, d), lambda i, j: (i, 0), pipeline_mode=pl.Buffered(1)),
        pl.BlockSpec((None, None, 1, d), lambda i, j: (layer, gain_idx, 0, 0)),
        pl.BlockSpec((None, None, d, tf), lambda i, j: (layer, which, 0, j)),
        pl.BlockSpec((None, None, d, tf), lambda i, j: (layer, which, 0, j)),
        pl.BlockSpec((None, None, tf, d), lambda i, j: (layer, which, j, 0)),
    ]
    args = [x, norm_g, w1, w3, w2]
    out_shape = [jax.ShapeDtypeStruct((n, d), F32)]
    out_specs = [pl.BlockSpec((tm, d), lambda i, j: (i, 0))]
    if with_next:
        in_specs.append(pl.BlockSpec((None, None, 1, d), lambda i, j: (layer, next_gain_idx, 0, 0)))
        args.append(norm_g)
        out_shape.append(jax.ShapeDtypeStruct((n, d), BF16))
        out_specs.append(pl.BlockSpec((tm, d), lambda i, j: (i, 0)))
    res = pl.pallas_call(
        functools.partial(_ffn_kernel, with_next=with_next),
        out_shape=out_shape,
        grid=(n // tm, f // tf),
        in_specs=in_specs,
        out_specs=out_specs,
        scratch_shapes=[pltpu.VMEM((tm, d), BF16)],
        compiler_params=_params(("parallel", "arbitrary")),
        name="ffn_next" if with_next else "ffn",
    )(*args)
    return res if with_next else res[0]


def _qkv_proj_kernel(*refs, norm):
    if norm:
        x_ref, w_ref, gain_ref, o_ref = refs
    else:
        x_ref, w_ref, o_ref = refs
    x = x_ref[...]
    ncols = o_ref.shape[-1]
    for c in range(ncols // PROJ_TN):
        h = jnp.dot(x, w_ref[:, c * PROJ_TN:(c + 1) * PROJ_TN], preferred_element_type=F32)
        for hh in range(PROJ_TN // HEAD_DIM):
            lo = c * PROJ_TN + hh * HEAD_DIM
            hs = h[:, hh * HEAD_DIM:(hh + 1) * HEAD_DIM]
            if norm:
                hs = _rms(hs, gain_ref[...])
            o_ref[:, lo:lo + HEAD_DIM] = hs.astype(BF16)


def _qkv_proj(xn, w_in, layer_idx, slab, gain=None):
    n, d = xn.shape
    width = 3 * A_WIDTH
    tm = PROJ_TM
    norm = gain is not None
    in_specs = [pl.BlockSpec((tm, d), lambda i: (i, 0)),
                _resident((None, d, width), lambda i: (layer_idx, 0, slab))]
    args = [xn, w_in]
    if norm:
        in_specs.append(pl.BlockSpec((None, 1, HEAD_DIM), lambda i: (layer_idx, 0, 0)))
        args.append(gain)
    return pl.pallas_call(
        functools.partial(_qkv_proj_kernel, norm=norm),
        out_shape=jax.ShapeDtypeStruct((n, width), BF16),
        grid=(n // tm,),
        in_specs=in_specs,
        out_specs=pl.BlockSpec((tm, width), lambda i: (i, 0)),
        compiler_params=_params(("parallel",)),
        name="qkv_proj_norm" if norm else "qkv_proj",
    )(*args)


def _carry_history(buf, halo, tm, first):
    @pl.when(first)
    def _():
        buf[0:halo, :] = jnp.zeros((halo, buf.shape[-1]), buf.dtype)

    @pl.when(jnp.logical_not(first))
    def _():
        buf[0:halo, :] = buf[tm:tm + halo, :]


def _short_conv_kernel(x_ref, w_ref, cw_ref, o_ref, work_ref, *, tiles_per_seq):
    tm = x_ref.shape[0]
    width = o_ref.shape[-1]
    _carry_history(work_ref, SUBLANES, tm, (pl.program_id(0) % tiles_per_seq) == 0)
    x = x_ref[...]
    for c in range(width // PROJ_TN):
        cols = slice(c * PROJ_TN, (c + 1) * PROJ_TN)

        def proj(k):
            lo = k * width + c * PROJ_TN
            return jnp.dot(x, w_ref[:, lo:lo + PROJ_TN], preferred_element_type=F32)

        bg, cg, xt = proj(0), proj(1), proj(2)
        u = cg * xt
        work_ref[SUBLANES:SUBLANES + tm, cols] = u
        y = (cw_ref[0:1, cols] * work_ref[SUBLANES - 2:SUBLANES - 2 + tm, cols]
             + cw_ref[1:2, cols] * work_ref[SUBLANES - 1:SUBLANES - 1 + tm, cols]
             + cw_ref[2:3, cols] * u)
        o_ref[:, cols] = (bg * y).astype(BF16)


def _short_conv(xn, w_in, conv_w, layer_idx, seq):
    n, d = xn.shape
    width = d - A_WIDTH
    tm = PROJ_TM
    return pl.pallas_call(
        functools.partial(_short_conv_kernel, tiles_per_seq=seq // tm),
        out_shape=jax.ShapeDtypeStruct((n, width), BF16),
        grid=(n // tm,),
        in_specs=[
            pl.BlockSpec((tm, d), lambda i: (i, 0)),
            _resident((None, d, 3 * width), lambda i: (layer_idx, 0, 3)),
            pl.BlockSpec((None, 3, width), lambda i: (layer_idx, 0, 0)),
        ],
        out_specs=pl.BlockSpec((tm, width), lambda i: (i, 0)),
        scratch_shapes=[pltpu.VMEM((SUBLANES + tm, width), F32)],
        compiler_params=_params(("arbitrary",)),
        name="short_conv",
    )(xn, w_in, conv_w)


def _attn_kernel(q_ref, kc_ref, kp_ref, vc_ref, vp_ref, o_ref, lse_ref, kbuf, vbuf, *, dilation):
    tl = q_ref.shape[0]
    blk = ATTN_BLOCK
    lt = pl.program_id(2)
    kbuf[0:blk, :] = kp_ref[...]
    kbuf[blk:blk + tl, :] = kc_ref[...]
    vbuf[0:blk, :] = vp_ref[...]
    vbuf[blk:blk + tl, :] = vc_ref[...]

    qi = lax.broadcasted_iota(jnp.int32, (blk, 2 * blk), 0)
    ci = lax.broadcasted_iota(jnp.int32, (blk, 2 * blk), 1)
    dist = qi + blk - ci
    valid = (dist >= 0) & (dist <= blk)
    true_dist = (dist * dilation).astype(F32)
    lane = lax.broadcasted_iota(jnp.int32, (blk, LANES), 1)
    scale = HEAD_DIM ** -0.5
    neg_inf = F32(-jnp.inf)

    for i in range(tl // blk):
        rows = slice(i * blk, (i + 1) * blk)
        lse_tile = jnp.zeros((blk, LANES), F32)
        for h in range(N_HEADS):
            hc = slice(h * HEAD_DIM, (h + 1) * HEAD_DIM)
            slope = float(2.0 ** (-8.0 * (h + 1) / N_HEADS))
            q = q_ref[rows, hc]
            kk = kbuf[i * blk:(i + 2) * blk, hc]
            vv = vbuf[i * blk:(i + 2) * blk, hc]
            s = lax.dot_general(q, kk, (((1,), (1,)), ((), ())), preferred_element_type=F32)
            s = s * scale + jnp.where(valid, -slope * true_dist, neg_inf)
            if i == 0:
                s = jnp.where((lt > 0) | (ci >= blk), s, neg_inf)
            m = jnp.max(s, axis=-1, keepdims=True)
            p = jnp.exp(s - m)
            l = jnp.sum(p, axis=-1, keepdims=True)
            o = jnp.dot(p.astype(BF16), vv, preferred_element_type=F32)
            o_ref[rows, hc] = o * (1.0 / l)
            lse_tile = jnp.where(lane == h, m + jnp.log(l), lse_tile)
        lse_ref[rows, :] = lse_tile


def _attention_group(q, k, v, batch, seq, group):
    d = DILATIONS[group]
    assert WINDOWS[group] // d == ATTN_BLOCK
    n = batch * seq
    sub = seq // d
    tl = min(ATTN_TL, sub)
    blk = ATTN_BLOCK
    ngrp = len(DILATIONS)
    qv = q.reshape(batch, sub, d * ngrp * A_WIDTH)
    kv = k.reshape(batch, sub, d * ngrp * A_WIDTH)
    vv = v.reshape(batch, sub, d * ngrp * A_WIDTH)

    def cur(b, r, lt):
        return (b, lt, r * ngrp + group)

    def prev(b, r, lt):
        return (b, jnp.maximum(lt * (tl // blk) - 1, 0), r * ngrp + group)

    o, lse = pl.pallas_call(
        functools.partial(_attn_kernel, dilation=d),
        out_shape=[jax.ShapeDtypeStruct((batch, sub, d * A_WIDTH), F32),
                   jax.ShapeDtypeStruct((batch, sub, d * LANES), F32)],
        grid=(batch, d, sub // tl),
        in_specs=[
            pl.BlockSpec((None, tl, A_WIDTH), cur),
            pl.BlockSpec((None, tl, A_WIDTH), cur),
            pl.BlockSpec((None, blk, A_WIDTH), prev),
            pl.BlockSpec((None, tl, A_WIDTH), cur),
            pl.BlockSpec((None, blk, A_WIDTH), prev),
        ],
        out_specs=[pl.BlockSpec((None, tl, A_WIDTH), lambda b, r, lt: (b, lt, r)),
                   pl.BlockSpec((None, tl, LANES), lambda b, r, lt: (b, lt, r))],
        scratch_shapes=[pltpu.VMEM((blk + tl, A_WIDTH), BF16),
                        pltpu.VMEM((blk + tl, A_WIDTH), BF16)],
        compiler_params=_params(("parallel", "parallel", "parallel")),
        name=f"attn_d{d}",
    )(qv, kv, kv, vv, vv)
    return o.reshape(n, A_WIDTH), lse.reshape(n, LANES)


def _even_out_kernel(x_ref, o0_ref, o1_ref, o2_ref, l0_ref, l1_ref, l2_ref, yb_ref, w_ref, out_ref):
    lses = [l0_ref[...], l1_ref[...], l2_ref[...]]
    mx = jnp.maximum(jnp.maximum(lses[0], lses[1]), lses[2])
    es = [jnp.exp(l - mx) for l in lses]
    inv = 1.0 / (es[0] + es[1] + es[2])
    alphas = [e * inv for e in es]
    o_refs = (o0_ref, o1_ref, o2_ref)
    parts = []
    for h in range(N_HEADS):
        hc = slice(h * HEAD_DIM, (h + 1) * HEAD_DIM)
        y = alphas[0][:, h:h + 1] * o_refs[0][:, hc]
        for g in (1, 2):
            y = y + alphas[g][:, h:h + 1] * o_refs[g][:, hc]
        parts.append(y.astype(BF16))
    ya = jnp.concatenate(parts, axis=-1)
    acc = jnp.dot(ya, w_ref[0:A_WIDTH, :], preferred_element_type=F32)
    acc = acc + jnp.dot(yb_ref[...], w_ref[A_WIDTH:, :], preferred_element_type=F32)
    out_ref[...] = x_ref[...] + acc


def _even_out(x, outs, lses, yb, w_out, layer_idx):
    n, d = x.shape
    tm = PROJ_TM
    row = lambda i: (i, 0)
    return pl.pallas_call(
        _even_out_kernel,
        out_shape=jax.ShapeDtypeStruct((n, d), F32),
        grid=(n // tm,),
        in_specs=[pl.BlockSpec((tm, d), row)]
        + [pl.BlockSpec((tm, A_WIDTH), row)] * 3
        + [pl.BlockSpec((tm, LANES), row)] * 3
        + [pl.BlockSpec((tm, d - A_WIDTH), row),
           _resident((None, d, d), lambda i: (layer_idx, 0, 0))],
        out_specs=pl.BlockSpec((tm, d), row),
        compiler_params=_params(("parallel",)),
        name="even_out",
    )(x, *outs, *lses, yb, w_out)


def _odd_mixer_kernel(x_ref, w_ref, cw_ref, cb_ref, lg_ref, lb_ref, pw_ref, ps_ref,
                      u_ref, yd_ref, ubuf, zbuf, cbuf, *, tiles_per_seq, conv_taps):
    tm = x_ref.shape[0]
    cw_width = u_ref.shape[-1]
    dw_width = yd_ref.shape[-1]
    i = pl.program_id(0)
    first = (i % tiles_per_seq) == 0
    x = x_ref[...]
    _carry_history(ubuf, CONV_HALO, tm, first)
    _carry_history(zbuf, POOL_HALO, tm, first)

    for c in range(cw_width // PROJ_TN):
        cols = slice(c * PROJ_TN, (c + 1) * PROJ_TN)
        a = jnp.dot(x, w_ref[:, c * PROJ_TN:(c + 1) * PROJ_TN], preferred_element_type=F32)
        gate = jnp.dot(x, w_ref[:, cw_width + c * PROJ_TN:cw_width + (c + 1) * PROJ_TN],
                       preferred_element_type=F32)
        ubuf[CONV_HALO:CONV_HALO + tm, cols] = a * jax.nn.sigmoid(gate)
    for c in range(dw_width // PROJ_TN):
        cols = slice(c * PROJ_TN, (c + 1) * PROJ_TN)
        lo = 2 * cw_width + c * PROJ_TN
        zbuf[POOL_HALO:POOL_HALO + tm, cols] = jnp.dot(x, w_ref[:, lo:lo + PROJ_TN],
                                                       preferred_element_type=F32)

    base = CONV_HALO - (conv_taps - 1)
    rc = 128
    for r0 in range(0, tm, rc):
        for c0 in range(0, cw_width, LANES):
            acc = jnp.zeros((rc, LANES), F32)
            for j in range(conv_taps):
                acc = acc + cw_ref[j:j + 1, c0:c0 + LANES] * ubuf[base + r0 + j:base + r0 + j + rc, c0:c0 + LANES]
            cbuf[r0:r0 + rc, c0:c0 + LANES] = acc + cb_ref[:, c0:c0 + LANES]

    cv = cbuf[...]
    mu = jnp.mean(cv, axis=-1, keepdims=True)
    cen = cv - mu
    var = jnp.mean(cen * cen, axis=-1, keepdims=True)
    yn = cen * lax.rsqrt(var + EPS) * lg_ref[...] + lb_ref[...]
    u_ref[...] = (yn * jax.nn.sigmoid(yn)).astype(BF16)

    t_in_seq = (i % tiles_per_seq) * tm + lax.broadcasted_iota(jnp.int32, (tm, 1), 0)
    t1 = (t_in_seq + 1).astype(F32)
    gw = dw_width // len(POOL_WINDOWS)
    for g, kw in enumerate(POOL_WINDOWS):
        cols = slice(g * gw, (g + 1) * gw)
        z = zbuf[POOL_HALO:POOL_HALO + tm, cols]
        tot = z
        for k in range(1, kw):
            tot = tot + zbuf[POOL_HALO - k:POOL_HALO - k + tm, cols]
        pooled = tot * (1.0 / jnp.minimum(t1, float(kw))) - z
        y = jnp.dot(pooled.astype(BF16), pw_ref[g], preferred_element_type=F32)
        yd_ref[:, cols] = (y * ps_ref[:, cols]).astype(BF16)


def _odd_mixer(xn, w_in, conv_w, conv_b, ln_g, ln_b, pool_w, pool_scale, layer_idx, seq):
    n, d = xn.shape
    cw_width = conv_w.shape[-1]
    dw_width = pool_scale.shape[-1]
    taps = conv_w.shape[-2]
    tm = PROJ_TM
    ngroups, gw = pool_w.shape[1], pool_w.shape[2]
    vec = lambda width: pl.BlockSpec((None, 1, width), lambda i: (layer_idx, 0, 0))
    return pl.pallas_call(
        functools.partial(_odd_mixer_kernel, tiles_per_seq=seq // tm, conv_taps=taps),
        out_shape=[jax.ShapeDtypeStruct((n, cw_width), BF16),
                   jax.ShapeDtypeStruct((n, dw_width), BF16)],
        grid=(n // tm,),
        in_specs=[
            pl.BlockSpec((tm, d), lambda i: (i, 0)),
            _resident((None, d, 2 * cw_width + dw_width), lambda i: (layer_idx, 0, 0)),
            pl.BlockSpec((None, taps, cw_width), lambda i: (layer_idx, 0, 0)),
            vec(cw_width), vec(cw_width), vec(cw_width),
            pl.BlockSpec((None, ngroups, gw, gw), lambda i: (layer_idx, 0, 0, 0)),
            vec(dw_width),
        ],
        out_specs=[pl.BlockSpec((tm, cw_width), lambda i: (i, 0)),
                   pl.BlockSpec((tm, dw_width), lambda i: (i, 0))],
        scratch_shapes=[pltpu.VMEM((CONV_HALO + tm, cw_width), F32),
                        pltpu.VMEM((POOL_HALO + tm, dw_width), F32),
                        pltpu.VMEM((tm, cw_width), F32)],
        compiler_params=_params(("arbitrary",)),
        name="odd_mixer",
    )(xn, w_in, conv_w, conv_b, ln_g, ln_b, pool_w, pool_scale)


def _odd_out_kernel(x_ref, ya_ref, yb_ref, w_ref, out_ref):
    half = ya_ref.shape[-1]
    acc = jnp.dot(ya_ref[...], w_ref[0:half, :], preferred_element_type=F32)
    acc = acc + jnp.dot(yb_ref[...], w_ref[half:, :], preferred_element_type=F32)
    out_ref[...] = x_ref[...] + acc


def _odd_out(x, ya, yb, w_out, layer_idx):
    n, d = x.shape
    tm = PROJ_TM
    row = lambda i: (i, 0)
    return pl.pallas_call(
        _odd_out_kernel,
        out_shape=jax.ShapeDtypeStruct((n, d), F32),
        grid=(n // tm,),
        in_specs=[pl.BlockSpec((tm, d), row),
                  pl.BlockSpec((tm, ya.shape[-1]), row),
                  pl.BlockSpec((tm, yb.shape[-1]), row),
                  _resident((None, d, d), lambda i: (layer_idx, 0, 0))],
        out_specs=pl.BlockSpec((tm, d), row),
        compiler_params=_params(("parallel",)),
        name="odd_out",
    )(x, ya, yb, w_out)


def kernel(x, norm_g, ffn_w1, ffn_w3, ffn_w2, ev_w_in, ev_q_gain, ev_k_gain, ev_conv_w, ev_w_out,
           od_w_in, od_conv_w, od_conv_b, od_ln_g, od_ln_b, od_pool_w, od_pool_scale, od_w_out):
    batch, seq, d = x.shape
    depth = norm_g.shape[0]
    n = batch * seq
    assert d - A_WIDTH == ev_conv_w.shape[-1] and seq % (DILATIONS[-1] * ATTN_BLOCK) == 0
    assert n % FFN_TM == 0 and seq % PROJ_TM == 0 and ffn_w1.shape[-1] % FFN_TF == 0

    w1, w3, w2 = (w.astype(BF16) for w in (ffn_w1, ffn_w3, ffn_w2))
    ev_in, ev_out, od_in, od_out = (w.astype(BF16) for w in (ev_w_in, ev_w_out, od_w_in, od_w_out))
    pool_w = od_pool_w.astype(BF16)
    gains = norm_g[:, :, None, :]
    row = lambda v: v[:, None, :]
    q_gain, k_gain = row(ev_q_gain), row(ev_k_gain)
    conv_b, ln_g, ln_b, pool_scale = row(od_conv_b), row(od_ln_g), row(od_ln_b), row(od_pool_scale)

    h = x.reshape(n, d)
    for layer in range(depth):
        idx = layer // 2
        h, xn = _ffn(h, gains, w1, w3, w2, layer, 0, 0, next_gain_idx=1)
        if layer % 2 == 0:
            q = _qkv_proj(xn, ev_in, idx, 0, gain=q_gain)
            k = _qkv_proj(xn, ev_in, idx, 1, gain=k_gain)
            v = _qkv_proj(xn, ev_in, idx, 2)
            yb = _short_conv(xn, ev_in, ev_conv_w, idx, seq)
            outs, lses = zip(*[_attention_group(q, k, v, batch, seq, g) for g in range(len(DILATIONS))])
            h = _even_out(h, outs, lses, yb, ev_out, idx)
        else:
            u, yd = _odd_mixer(xn, od_in, od_conv_w, conv_b, ln_g, ln_b, pool_w, pool_scale, idx, seq)
            h = _odd_out(h, u, yd, od_out, idx)
        h = _ffn(h, gains, w1, w3, w2, layer, 1, 2)
    return h.reshape(batch, seq, d)
```

```python
import functools

import numpy as np
import jax
import jax.numpy as jnp
from jax import lax
from jax.experimental import pallas as pl
from jax.experimental.pallas import tpu as pltpu

F32 = jnp.float32
BF16 = jnp.bfloat16
EPS = 1e-6

HEAD_DIM = 128
N_HEADS = 8
A_WIDTH = N_HEADS * HEAD_DIM
WINDOWS = (128, 512, 2048)
DILATIONS = (1, 4, 16)
ATTN_BLOCK = 128
POOL_WINDOWS = (2, 4, 8, 16)

LANES = 128
SUBLANES = 8
VMEM_LIMIT = 60000 * 1024
FFN_TM = 1024
FFN_TF = 512
PROJ_TM = 512
PROJ_TN = 512
ATTN_TL = (512, 128, 128)
ATTN_HEADS = 4
CONV_HALO = 32
CONV_ROWS = 128
POOL_HALO = 16


def _params(semantics):
    return pltpu.CompilerParams(dimension_semantics=semantics, vmem_limit_bytes=VMEM_LIMIT)


def _resident(block_shape, index_map):
    return pl.BlockSpec(block_shape, index_map, pipeline_mode=pl.Buffered(1))


def _rms(x, gain):
    ms = jnp.mean(x * x, axis=-1, keepdims=True)
    return x * lax.rsqrt(ms + EPS) * gain


def _ffn_kernel(*refs, with_next):
    if with_next:
        x_ref, g_ref, w1_ref, w3_ref, w2_ref, gn_ref, o_ref, xn_ref, xs_ref = refs
    else:
        x_ref, g_ref, w1_ref, w3_ref, w2_ref, o_ref, xs_ref = refs
    j = pl.program_id(1)

    @pl.when(j == 0)
    def _():
        xf = x_ref[...]
        xs_ref[...] = _rms(xf, g_ref[...]).astype(BF16)
        o_ref[...] = xf

    xn = xs_ref[...]
    h1 = jnp.dot(xn, w1_ref[...], preferred_element_type=F32)
    h3 = jnp.dot(xn, w3_ref[...], preferred_element_type=F32)
    a = (0.5 * (h1 * jax.nn.sigmoid(h1)) * h3).astype(BF16)
    o_ref[...] += jnp.dot(a, w2_ref[...], preferred_element_type=F32)

    if with_next:
        @pl.when(j == pl.num_programs(1) - 1)
        def _():
            xn_ref[...] = _rms(o_ref[...], gn_ref[...]).astype(BF16)


def _ffn(x, norm_g, w1, w3, w2, layer, which, gain_idx, next_gain_idx=None):
    n, d = x.shape
    nf, tf = w1.shape[2], w1.shape[4]
    tm = FFN_TM
    with_next = next_gain_idx is not None
    in_specs = [
        pl.BlockSpec((tm, d), lambda i, j: (i, 0), pipeline_mode=pl.Buffered(1)),
        pl.BlockSpec((None, None, 1, d), lambda i, j: (layer, gain_idx, 0, 0)),
        pl.BlockSpec((None, None, None, d, tf), lambda i, j: (layer, which, j, 0, 0)),
        pl.BlockSpec((None, None, None, d, tf), lambda i, j: (layer, which, j, 0, 0)),
        pl.BlockSpec((None, None, tf, d), lambda i, j: (layer, which, j, 0)),
    ]
    args = [x, norm_g, w1, w3, w2]
    out_shape = [jax.ShapeDtypeStruct((n, d), F32)]
    out_specs = [pl.BlockSpec((tm, d), lambda i, j: (i, 0))]
    if with_next:
        in_specs.append(pl.BlockSpec((None, None, 1, d), lambda i, j: (layer, next_gain_idx, 0, 0)))
        args.append(norm_g)
        out_shape.append(jax.ShapeDtypeStruct((n, d), BF16))
        out_specs.append(pl.BlockSpec((tm, d), lambda i, j: (i, 0)))
    res = pl.pallas_call(
        functools.partial(_ffn_kernel, with_next=with_next),
        out_shape=out_shape,
        grid=(n // tm, nf),
        in_specs=in_specs,
        out_specs=out_specs,
        scratch_shapes=[pltpu.VMEM((tm, d), BF16)],
        compiler_params=_params(("parallel", "arbitrary")),
        name="ffn_next" if with_next else "ffn",
    )(*args)
    return res if with_next else res[0]


def _group_proj_kernel(x_ref, wq_ref, wk_ref, wv_ref, qg_ref, kg_ref, q_out, k_out, v_out):
    x = x_ref[...]
    for w_ref, gain_ref, o_ref in ((wq_ref, qg_ref, q_out), (wk_ref, kg_ref, k_out), (wv_ref, None, v_out)):
        for c in range(A_WIDTH // PROJ_TN):
            h = jnp.dot(x, w_ref[:, c * PROJ_TN:(c + 1) * PROJ_TN], preferred_element_type=F32)
            for hh in range(PROJ_TN // HEAD_DIM):
                lo = c * PROJ_TN + hh * HEAD_DIM
                hs = h[:, hh * HEAD_DIM:(hh + 1) * HEAD_DIM]
                if gain_ref is not None:
                    hs = _rms(hs, gain_ref[...])
                o_ref[:, lo:lo + HEAD_DIM] = hs.astype(BF16)


def _group_proj(xg, w_in, q_gain, k_gain, layer_idx, group):
    n, d = xg.shape
    tm = PROJ_TM
    ngrp = len(DILATIONS)
    slab = lambda which: _resident((None, d, A_WIDTH), lambda i: (layer_idx, 0, which * ngrp + group))
    gain = pl.BlockSpec((None, 1, HEAD_DIM), lambda i: (layer_idx, 0, 0))
    out = pl.BlockSpec((tm, A_WIDTH), lambda i: (i, 0))
    return pl.pallas_call(
        _group_proj_kernel,
        out_shape=[jax.ShapeDtypeStruct((n, A_WIDTH), BF16)] * 3,
        grid=(n // tm,),
        in_specs=[pl.BlockSpec((tm, d), lambda i: (i, 0)), slab(0), slab(1), slab(2), gain, gain],
        out_specs=[out, out, out],
        compiler_params=_params(("parallel",)),
        name="group_proj",
    )(xg, w_in, w_in, w_in, q_gain, k_gain)


def _carry_history(buf, halo, tm, first):
    @pl.when(first)
    def _():
        buf[0:halo, :] = jnp.zeros((halo, buf.shape[-1]), buf.dtype)

    @pl.when(jnp.logical_not(first))
    def _():
        buf[0:halo, :] = buf[tm:tm + halo, :]


def _short_conv_kernel(x_ref, w_ref, cw_ref, o_ref, work_ref, *, tiles_per_seq):
    tm = x_ref.shape[0]
    width = o_ref.shape[-1]
    _carry_history(work_ref, SUBLANES, tm, (pl.program_id(0) % tiles_per_seq) == 0)
    x = x_ref[...]
    for c in range(width // PROJ_TN):
        cols = slice(c * PROJ_TN, (c + 1) * PROJ_TN)

        def proj(k):
            lo = k * width + c * PROJ_TN
            return jnp.dot(x, w_ref[:, lo:lo + PROJ_TN], preferred_element_type=F32)

        bg, cg, xt = proj(0), proj(1), proj(2)
        u = cg * xt
        work_ref[SUBLANES:SUBLANES + tm, cols] = u
        y = (cw_ref[0:1, cols] * work_ref[SUBLANES - 2:SUBLANES - 2 + tm, cols]
             + cw_ref[1:2, cols] * work_ref[SUBLANES - 1:SUBLANES - 1 + tm, cols]
             + cw_ref[2:3, cols] * u)
        o_ref[:, cols] = (bg * y).astype(BF16)


def _short_conv(xn, w_in, conv_w, layer_idx, seq):
    n, d = xn.shape
    width = d - A_WIDTH
    tm = PROJ_TM
    return pl.pallas_call(
        functools.partial(_short_conv_kernel, tiles_per_seq=seq // tm),
        out_shape=jax.ShapeDtypeStruct((n, width), BF16),
        grid=(n // tm,),
        in_specs=[
            pl.BlockSpec((tm, d), lambda i: (i, 0)),
            _resident((None, d, 3 * width), lambda i: (layer_idx, 0, 3)),
            pl.BlockSpec((None, 3, width), lambda i: (layer_idx, 0, 0)),
        ],
        out_specs=pl.BlockSpec((tm, width), lambda i: (i, 0)),
        scratch_shapes=[pltpu.VMEM((SUBLANES + tm, width), F32)],
        compiler_params=_params(("arbitrary",)),
        name="short_conv",
    )(xn, w_in, conv_w)


def _attn_kernel(slopes_ref, q_ref, kc_ref, kp_ref, vc_ref, vp_ref, o_ref, lse_ref):
    d, tl, hw = q_ref.shape
    blk = ATTN_BLOCK
    heads = hw // HEAD_DIM
    lt = pl.program_id(1)
    head0 = pl.program_id(2) * heads

    qi = lax.broadcasted_iota(jnp.int32, (blk, 2 * blk), 0)
    ci = lax.broadcasted_iota(jnp.int32, (blk, 2 * blk), 1)
    dist = qi + blk - ci
    valid = (dist >= 0) & (dist <= blk)
    neg_inf = F32(-jnp.inf)
    neg_dist = jnp.where(valid, -(dist * d).astype(F32), neg_inf)
    has_prev = (lt > 0) | (ci >= blk)
    lane = lax.broadcasted_iota(jnp.int32, (blk, LANES), 1)
    scale = HEAD_DIM ** -0.5

    for r in range(d):
        for i in range(tl // blk):
            if d == 1:
                rows = pl.ds(i * blk, blk)
            else:
                rows = pl.ds(i * blk * d + r, blk, stride=d)
            lse_tile = jnp.zeros((blk, LANES), F32)
            for h in range(heads):
                hc = slice(h * HEAD_DIM, (h + 1) * HEAD_DIM)
                q = q_ref[r, i * blk:(i + 1) * blk, hc]
                if i == 0:
                    kk = jnp.concatenate([kp_ref[r, :, hc], kc_ref[r, 0:blk, hc]], axis=0)
                    vv = jnp.concatenate([vp_ref[r, :, hc], vc_ref[r, 0:blk, hc]], axis=0)
                else:
                    kk = kc_ref[r, (i - 1) * blk:(i + 1) * blk, hc]
                    vv = vc_ref[r, (i - 1) * blk:(i + 1) * blk, hc]
                s = lax.dot_general(q, kk, (((1,), (1,)), ((), ())), preferred_element_type=F32)
                s = s * scale + slopes_ref[head0 + h] * neg_dist
                if i == 0:
                    s = jnp.where(has_prev, s, neg_inf)
                m = jnp.max(s, axis=-1, keepdims=True)
                p = jnp.exp(s - m)
                l = jnp.sum(p, axis=-1, keepdims=True)
                o = jnp.dot(p.astype(BF16), vv, preferred_element_type=F32)
                o_ref[h, rows, :] = o * (1.0 / l)
                lse_tile = jnp.where(lane == h, m + jnp.log(l), lse_tile)
            lse_ref[rows, :] = lse_tile


def _attention_group(slopes, q, k, v, batch, seq, group):
    d = DILATIONS[group]
    assert WINDOWS[group] // d == ATTN_BLOCK
    sub = seq // d
    tl = ATTN_TL[group]
    blk = ATTN_BLOCK
    hw = ATTN_HEADS * HEAD_DIM
    nchunk = N_HEADS // ATTN_HEADS
    span = tl * d
    shape = (batch, d, sub, A_WIDTH)
    qv, kv, vv = q.reshape(shape), k.reshape(shape), v.reshape(shape)
    cur = pl.BlockSpec((None, d, tl, hw), lambda b, lt, hx: (b, 0, lt, hx))
    prev = pl.BlockSpec((None, d, blk, hw),
                        lambda b, lt, hx: (b, 0, jnp.maximum(lt * (tl // blk) - 1, 0), hx))
    o, lse = pl.pallas_call(
        _attn_kernel,
        out_shape=[jax.ShapeDtypeStruct((N_HEADS, batch, seq, HEAD_DIM), F32),
                   jax.ShapeDtypeStruct((nchunk, batch, seq, LANES), F32)],
        grid=(batch, sub // tl, nchunk),
        in_specs=[pl.BlockSpec(memory_space=pltpu.SMEM), cur, cur, prev, cur, prev],
        out_specs=[pl.BlockSpec((ATTN_HEADS, None, span, HEAD_DIM), lambda b, lt, hx: (hx, b, lt, 0)),
                   pl.BlockSpec((None, None, span, LANES), lambda b, lt, hx: (hx, b, lt, 0))],
        compiler_params=_params(("parallel", "parallel", "parallel")),
        name=f"attn_d{d}",
    )(slopes, qv, kv, kv, vv, vv)
    n = batch * seq
    return o.reshape(N_HEADS, n, HEAD_DIM), lse.reshape(nchunk, n, LANES)


def _even_out_kernel(x_ref, o0_ref, o1_ref, o2_ref, l0_ref, l1_ref, l2_ref, yb_ref, w_ref, out_ref):
    lses = [l0_ref[...], l1_ref[...], l2_ref[...]]
    mx = jnp.maximum(jnp.maximum(lses[0], lses[1]), lses[2])
    es = [jnp.exp(l - mx) for l in lses]
    inv = 1.0 / (es[0] + es[1] + es[2])
    alphas = [e * inv for e in es]
    o_refs = (o0_ref, o1_ref, o2_ref)
    parts = []
    for h in range(N_HEADS):
        chunk, lane = divmod(h, ATTN_HEADS)
        y = alphas[0][chunk][:, lane:lane + 1] * o_refs[0][h]
        for g in (1, 2):
            y = y + alphas[g][chunk][:, lane:lane + 1] * o_refs[g][h]
        parts.append(y.astype(BF16))
    ya = jnp.concatenate(parts, axis=-1)
    acc = jnp.dot(ya, w_ref[0:A_WIDTH, :], preferred_element_type=F32)
    acc = acc + jnp.dot(yb_ref[...], w_ref[A_WIDTH:, :], preferred_element_type=F32)
    out_ref[...] = x_ref[...] + acc


def _even_out(x, outs, lses, yb, w_out, layer_idx):
    n, d = x.shape
    tm = PROJ_TM
    nchunk = N_HEADS // ATTN_HEADS
    row = lambda i: (i, 0)
    return pl.pallas_call(
        _even_out_kernel,
        out_shape=jax.ShapeDtypeStruct((n, d), F32),
        grid=(n // tm,),
        in_specs=[pl.BlockSpec((tm, d), row)]
        + [pl.BlockSpec((N_HEADS, tm, HEAD_DIM), lambda i: (0, i, 0))] * 3
        + [pl.BlockSpec((nchunk, tm, LANES), lambda i: (0, i, 0))] * 3
        + [pl.BlockSpec((tm, d - A_WIDTH), row),
           _resident((None, d, d), lambda i: (layer_idx, 0, 0))],
        out_specs=pl.BlockSpec((tm, d), row),
        compiler_params=_params(("parallel",)),
        name="even_out",
    )(x, *outs, *lses, yb, w_out)


def _odd_mixer_kernel(x_ref, w_ref, cw_ref, cb_ref, lg_ref, lb_ref, pw_ref, ps_ref,
                      u_ref, yd_ref, ubuf, zbuf, cbuf, *, tiles_per_seq, conv_taps):
    tm = x_ref.shape[0]
    cw_width = u_ref.shape[-1]
    dw_width = yd_ref.shape[-1]
    i = pl.program_id(0)
    first = (i % tiles_per_seq) == 0
    x = x_ref[...]
    _carry_history(ubuf, CONV_HALO, tm, first)
    _carry_history(zbuf, POOL_HALO, tm, first)

    for c in range(cw_width // PROJ_TN):
        cols = slice(c * PROJ_TN, (c + 1) * PROJ_TN)
        a = jnp.dot(x, w_ref[:, c * PROJ_TN:(c + 1) * PROJ_TN], preferred_element_type=F32)
        gate = jnp.dot(x, w_ref[:, cw_width + c * PROJ_TN:cw_width + (c + 1) * PROJ_TN],
                       preferred_element_type=F32)
        ubuf[CONV_HALO:CONV_HALO + tm, cols] = a * jax.nn.sigmoid(gate)
    for c in range(dw_width // PROJ_TN):
        cols = slice(c * PROJ_TN, (c + 1) * PROJ_TN)
        lo = 2 * cw_width + c * PROJ_TN
        zbuf[POOL_HALO:POOL_HALO + tm, cols] = jnp.dot(x, w_ref[:, lo:lo + PROJ_TN],
                                                       preferred_element_type=F32)

    lead = CONV_HALO - (conv_taps - 1)
    rc = CONV_ROWS
    for r0 in range(0, tm, rc):
        for c0 in range(0, cw_width, LANES):
            y = None
            for b in range(SUBLANES):
                zb = None
                for a in range(-(-(conv_taps + lead) // SUBLANES)):
                    j = SUBLANES * a + b - lead
                    if 0 <= j < conv_taps:
                        lo = r0 + SUBLANES * a
                        term = (cw_ref[j:j + 1, c0:c0 + LANES]
                                * ubuf[lo:lo + rc + (SUBLANES if b else 0), c0:c0 + LANES])
                        zb = term if zb is None else zb + term
                zs = zb[b:b + rc] if b else zb
                y = zs if y is None else y + zs
            cbuf[r0:r0 + rc, c0:c0 + LANES] = y + cb_ref[:, c0:c0 + LANES]

    cv = cbuf[...]
    mu = jnp.mean(cv, axis=-1, keepdims=True)
    cen = cv - mu
    var = jnp.mean(cen * cen, axis=-1, keepdims=True)
    yn = cen * lax.rsqrt(var + EPS) * lg_ref[...] + lb_ref[...]
    u_ref[...] = (yn * jax.nn.sigmoid(yn)).astype(BF16)

    t_in_seq = (i % tiles_per_seq) * tm + lax.broadcasted_iota(jnp.int32, (tm, 1), 0)
    t1 = (t_in_seq + 1).astype(F32)
    gw = dw_width // len(POOL_WINDOWS)
    for g, kw in enumerate(POOL_WINDOWS):
        cols = slice(g * gw, (g + 1) * gw)
        z = zbuf[POOL_HALO:POOL_HALO + tm, cols]
        tot = z
        for k in range(1, kw):
            tot = tot + zbuf[POOL_HALO - k:POOL_HALO - k + tm, cols]
        pooled = tot * (1.0 / jnp.minimum(t1, float(kw))) - z
        y = jnp.dot(pooled.astype(BF16), pw_ref[g], preferred_element_type=F32)
        yd_ref[:, cols] = (y * ps_ref[:, cols]).astype(BF16)


def _odd_mixer(xn, w_in, conv_w, conv_b, ln_g, ln_b, pool_w, pool_scale, layer_idx, seq):
    n, d = xn.shape
    cw_width = conv_w.shape[-1]
    dw_width = pool_scale.shape[-1]
    taps = conv_w.shape[-2]
    assert taps - 1 <= CONV_HALO and max(POOL_WINDOWS) - 1 <= POOL_HALO
    tm = PROJ_TM
    ngroups, gw = pool_w.shape[1], pool_w.shape[2]
    vec = lambda width: pl.BlockSpec((None, 1, width), lambda i: (layer_idx, 0, 0))
    return pl.pallas_call(
        functools.partial(_odd_mixer_kernel, tiles_per_seq=seq // tm, conv_taps=taps),
        out_shape=[jax.ShapeDtypeStruct((n, cw_width), BF16),
                   jax.ShapeDtypeStruct((n, dw_width), BF16)],
        grid=(n // tm,),
        in_specs=[
            pl.BlockSpec((tm, d), lambda i: (i, 0)),
            _resident((None, d, 2 * cw_width + dw_width), lambda i: (layer_idx, 0, 0)),
            pl.BlockSpec((None, taps, cw_width), lambda i: (layer_idx, 0, 0)),
            vec(cw_width), vec(cw_width), vec(cw_width),
            pl.BlockSpec((None, ngroups, gw, gw), lambda i: (layer_idx, 0, 0, 0)),
            vec(dw_width),
        ],
        out_specs=[pl.BlockSpec((tm, cw_width), lambda i: (i, 0)),
                   pl.BlockSpec((tm, dw_width), lambda i: (i, 0))],
        scratch_shapes=[pltpu.VMEM((CONV_HALO + tm, cw_width), F32),
                        pltpu.VMEM((POOL_HALO + tm, dw_width), F32),
                        pltpu.VMEM((tm, cw_width), F32)],
        compiler_params=_params(("arbitrary",)),
        name="odd_mixer",
    )(xn, w_in, conv_w, conv_b, ln_g, ln_b, pool_w, pool_scale)


def _odd_out_kernel(x_ref, ya_ref, yb_ref, w_ref, out_ref):
    half = ya_ref.shape[-1]
    acc = jnp.dot(ya_ref[...], w_ref[0:half, :], preferred_element_type=F32)
    acc = acc + jnp.dot(yb_ref[...], w_ref[half:, :], preferred_element_type=F32)
    out_ref[...] = x_ref[...] + acc


def _odd_out(x, ya, yb, w_out, layer_idx):
    n, d = x.shape
    tm = PROJ_TM
    row = lambda i: (i, 0)
    return pl.pallas_call(
        _odd_out_kernel,
        out_shape=jax.ShapeDtypeStruct((n, d), F32),
        grid=(n // tm,),
        in_specs=[pl.BlockSpec((tm, d), row),
                  pl.BlockSpec((tm, ya.shape[-1]), row),
                  pl.BlockSpec((tm, yb.shape[-1]), row),
                  _resident((None, d, d), lambda i: (layer_idx, 0, 0))],
        out_specs=pl.BlockSpec((tm, d), row),
        compiler_params=_params(("parallel",)),
        name="odd_out",
    )(x, ya, yb, w_out)


def _by_residue(xn, batch, seq, dilation):
    if dilation == 1:
        return xn
    d = xn.shape[-1]
    x4 = xn.reshape(batch, seq // dilation, dilation, d)
    return jnp.swapaxes(x4, 1, 2).reshape(batch * seq, d)


def kernel(x, norm_g, ffn_w1, ffn_w3, ffn_w2, ev_w_in, ev_q_gain, ev_k_gain, ev_conv_w, ev_w_out,
           od_w_in, od_conv_w, od_conv_b, od_ln_g, od_ln_b, od_pool_w, od_pool_scale, od_w_out):
    batch, seq, d = x.shape
    depth = norm_g.shape[0]
    n = batch * seq
    f = ffn_w1.shape[-1]
    assert d - A_WIDTH == ev_conv_w.shape[-1] and seq % (DILATIONS[-1] * ATTN_BLOCK) == 0
    assert n % FFN_TM == 0 and seq % PROJ_TM == 0 and f % FFN_TF == 0
    assert all((seq // dl) % tl == 0 for dl, tl in zip(DILATIONS, ATTN_TL))

    def col_tiles(w):
        w = w.astype(BF16).reshape(depth, 2, d, f // FFN_TF, FFN_TF)
        return jnp.swapaxes(w, 2, 3)

    w1, w3, w2 = col_tiles(ffn_w1), col_tiles(ffn_w3), ffn_w2.astype(BF16)
    ev_in, ev_out, od_in, od_out = (w.astype(BF16) for w in (ev_w_in, ev_w_out, od_w_in, od_w_out))
    pool_w = od_pool_w.astype(BF16)
    gains = norm_g[:, :, None, :]
    row = lambda v: v[:, None, :]
    q_gain, k_gain = row(ev_q_gain), row(ev_k_gain)
    conv_b, ln_g, ln_b, pool_scale = row(od_conv_b), row(od_ln_g), row(od_ln_b), row(od_pool_scale)
    slopes = jnp.asarray(2.0 ** (-8.0 * np.arange(1, N_HEADS + 1) / N_HEADS), dtype=F32)

    h = x.reshape(n, d)
    for layer in range(depth):
        idx = layer // 2
        h, xn = _ffn(h, gains, w1, w3, w2, layer, 0, 0, next_gain_idx=1)
        if layer % 2 == 0:
            outs, lses = [], []
            for g, dl in enumerate(DILATIONS):
                q, k, v = _group_proj(_by_residue(xn, batch, seq, dl), ev_in, q_gain, k_gain, idx, g)
                o, lse = _attention_group(slopes, q, k, v, batch, seq, g)
                outs.append(o)
                lses.append(lse)
            yb = _short_conv(xn, ev_in, ev_conv_w, idx, seq)
            h = _even_out(h, outs, lses, yb, ev_out, idx)
        else:
            u, yd = _odd_mixer(xn, od_in, od_conv_w, conv_b, ln_g, ln_b, pool_w, pool_scale, idx, seq)
            h = _odd_out(h, u, yd, od_out, idx)
        h = _ffn(h, gains, w1, w3, w2, layer, 1, 2)
    return h.reshape(batch, seq, d)
```

```python
import functools

import numpy as np
import jax
import jax.numpy as jnp
from jax import lax
from jax.experimental import pallas as pl
from jax.experimental.pallas import tpu as pltpu

F32 = jnp.float32
BF16 = jnp.bfloat16
EPS = 1e-6

HEAD_DIM = 128
N_HEADS = 8
A_WIDTH = N_HEADS * HEAD_DIM
WINDOWS = (128, 512, 2048)
DILATIONS = (1, 4, 16)
ATTN_BLOCK = 128
POOL_WINDOWS = (2, 4, 8, 16)

LANES = 128
SUBLANES = 8
VMEM_LIMIT = 60000 * 1024
FFN_TM = 1024
FFN_TF = 512
PROJ_TM = 512
PROJ_TN = 512
ATTN_TL = (512, 256, 128)
ATTN_HEADS = (8, 8, 4)
LSE_HEADS = 4
CONV_HALO = 32
CONV_ROWS = 128
POOL_HALO = 16


def _params(semantics):
    return pltpu.CompilerParams(dimension_semantics=semantics, vmem_limit_bytes=VMEM_LIMIT)


def _resident(block_shape, index_map):
    return pl.BlockSpec(block_shape, index_map, pipeline_mode=pl.Buffered(1))


def _rms(x, gain):
    ms = jnp.mean(x * x, axis=-1, keepdims=True)
    return x * lax.rsqrt(ms + EPS) * gain


def _ffn_kernel(*refs, with_next):
    if with_next:
        x_hbm, g_ref, w1_ref, w3_ref, w2_ref, gn_ref, o_ref, xn_ref, xs_ref, xbuf, xsem = refs
    else:
        x_hbm, g_ref, w1_ref, w3_ref, w2_ref, o_ref, xs_ref, xbuf, xsem = refs
    i, j = pl.program_id(0), pl.program_id(1)
    tm = xbuf.shape[0]

    def x_copy(tile):
        return pltpu.make_async_copy(x_hbm.at[pl.ds(tile * tm, tm), :], xbuf, xsem)

    @pl.when((i == 0) & (j == 0))
    def _():
        x_copy(0).start()

    @pl.when(j == 0)
    def _():
        x_copy(i).wait()
        xf = xbuf[...]
        xs_ref[...] = _rms(xf, g_ref[...]).astype(BF16)
        o_ref[...] = xf

    @pl.when((j == 1) & (i + 1 < pl.num_programs(0)))
    def _():
        x_copy(i + 1).start()

    xn = xs_ref[...]
    h1 = jnp.dot(xn, w1_ref[...], preferred_element_type=F32)
    h3 = jnp.dot(xn, w3_ref[...], preferred_element_type=F32)
    a = (0.5 * (h1 * jax.nn.sigmoid(h1)) * h3).astype(BF16)
    o_ref[...] += jnp.dot(a, w2_ref[...], preferred_element_type=F32)

    if with_next:
        @pl.when(j == pl.num_programs(1) - 1)
        def _():
            xn_ref[...] = _rms(o_ref[...], gn_ref[...]).astype(BF16)


def _ffn(x, norm_g, w1, w3, w2, layer, which, gain_idx, next_gain_idx=None):
    n, d = x.shape
    tm, tf = FFN_TM, FFN_TF
    nf = w1.shape[-1] // tf
    with_next = next_gain_idx is not None
    in_specs = [
        pl.BlockSpec(memory_space=pl.ANY),
        pl.BlockSpec((None, None, 1, d), lambda i, j: (layer, gain_idx, 0, 0)),
        pl.BlockSpec((None, None, d, tf), lambda i, j: (layer, which, 0, j)),
        pl.BlockSpec((None, None, d, tf), lambda i, j: (layer, which, 0, j)),
        pl.BlockSpec((None, None, tf, d), lambda i, j: (layer, which, j, 0)),
    ]
    args = [x, norm_g, w1, w3, w2]
    out_shape = [jax.ShapeDtypeStruct((n, d), F32)]
    out_specs = [pl.BlockSpec((tm, d), lambda i, j: (i, 0))]
    if with_next:
        in_specs.append(pl.BlockSpec((None, None, 1, d), lambda i, j: (layer, next_gain_idx, 0, 0)))
        args.append(norm_g)
        out_shape.append(jax.ShapeDtypeStruct((n, d), BF16))
        out_specs.append(pl.BlockSpec((tm, d), lambda i, j: (i, 0)))
    res = pl.pallas_call(
        functools.partial(_ffn_kernel, with_next=with_next),
        out_shape=out_shape,
        grid=(n // tm, nf),
        in_specs=in_specs,
        out_specs=out_specs,
        scratch_shapes=[pltpu.VMEM((tm, d), BF16), pltpu.VMEM((tm, d), F32), pltpu.SemaphoreType.DMA(())],
        compiler_params=_params(("arbitrary", "arbitrary")),
        name="ffn_next" if with_next else "ffn",
    )(*args)
    return res if with_next else res[0]


def _group_proj_kernel(x_ref, wq_ref, wk_ref, wv_ref, qg_ref, kg_ref, q_out, k_out, v_out):
    x = x_ref[...]
    for w_ref, gain_ref, o_ref in ((wq_ref, qg_ref, q_out), (wk_ref, kg_ref, k_out), (wv_ref, None, v_out)):
        for c in range(A_WIDTH // PROJ_TN):
            h = jnp.dot(x, w_ref[:, c * PROJ_TN:(c + 1) * PROJ_TN], preferred_element_type=F32)
            for hh in range(PROJ_TN // HEAD_DIM):
                lo = c * PROJ_TN + hh * HEAD_DIM
                hs = h[:, hh * HEAD_DIM:(hh + 1) * HEAD_DIM]
                if gain_ref is not None:
                    hs = _rms(hs, gain_ref[...])
                o_ref[:, lo:lo + HEAD_DIM] = hs.astype(BF16)


def _group_proj(xg, w_in, q_gain, k_gain, layer_idx, group):
    n, d = xg.shape
    tm = PROJ_TM
    ngrp = len(DILATIONS)
    slab = lambda which: _resident((None, d, A_WIDTH), lambda i: (layer_idx, 0, which * ngrp + group))
    gain = pl.BlockSpec((None, 1, HEAD_DIM), lambda i: (layer_idx, 0, 0))
    out = pl.BlockSpec((tm, A_WIDTH), lambda i: (i, 0))
    return pl.pallas_call(
        _group_proj_kernel,
        out_shape=[jax.ShapeDtypeStruct((n, A_WIDTH), BF16)] * 3,
        grid=(n // tm,),
        in_specs=[pl.BlockSpec((tm, d), lambda i: (i, 0)), slab(0), slab(1), slab(2), gain, gain],
        out_specs=[out, out, out],
        compiler_params=_params(("parallel",)),
        name="group_proj",
    )(xg, w_in, w_in, w_in, q_gain, k_gain)


def _carry_history(buf, halo, tm, first):
    @pl.when(first)
    def _():
        buf[0:halo, :] = jnp.zeros((halo, buf.shape[-1]), buf.dtype)

    @pl.when(jnp.logical_not(first))
    def _():
        buf[0:halo, :] = buf[tm:tm + halo, :]


def _short_conv_kernel(x_ref, w_ref, cw_ref, o_ref, work_ref, *, tiles_per_seq):
    tm = x_ref.shape[0]
    width = o_ref.shape[-1]
    _carry_history(work_ref, SUBLANES, tm, (pl.program_id(0) % tiles_per_seq) == 0)
    x = x_ref[...]
    for c in range(width // PROJ_TN):
        cols = slice(c * PROJ_TN, (c + 1) * PROJ_TN)

        def proj(k):
            lo = k * width + c * PROJ_TN
            return jnp.dot(x, w_ref[:, lo:lo + PROJ_TN], preferred_element_type=F32)

        bg, cg, xt = proj(0), proj(1), proj(2)
        u = cg * xt
        work_ref[SUBLANES:SUBLANES + tm, cols] = u
        y = (cw_ref[0:1, cols] * work_ref[SUBLANES - 2:SUBLANES - 2 + tm, cols]
             + cw_ref[1:2, cols] * work_ref[SUBLANES - 1:SUBLANES - 1 + tm, cols]
             + cw_ref[2:3, cols] * u)
        o_ref[:, cols] = (bg * y).astype(BF16)


def _short_conv(xn, w_in, conv_w, layer_idx, seq):
    n, d = xn.shape
    width = d - A_WIDTH
    tm = PROJ_TM
    return pl.pallas_call(
        functools.partial(_short_conv_kernel, tiles_per_seq=seq // tm),
        out_shape=jax.ShapeDtypeStruct((n, width), BF16),
        grid=(n // tm,),
        in_specs=[
            pl.BlockSpec((tm, d), lambda i: (i, 0)),
            _resident((None, d, 3 * width), lambda i: (layer_idx, 0, 3)),
            pl.BlockSpec((None, 3, width), lambda i: (layer_idx, 0, 0)),
        ],
        out_specs=pl.BlockSpec((tm, width), lambda i: (i, 0)),
        scratch_shapes=[pltpu.VMEM((SUBLANES + tm, width), F32)],
        compiler_params=_params(("arbitrary",)),
        name="short_conv",
    )(xn, w_in, conv_w)


def _attn_kernel(slopes_ref, q_ref, kc_ref, kp_ref, vc_ref, vp_ref, o_ref, lse_ref):
    d, tl, hw = q_ref.shape
    blk = ATTN_BLOCK
    heads = hw // HEAD_DIM
    lt = pl.program_id(1)
    head0 = pl.program_id(2) * heads

    qi = lax.broadcasted_iota(jnp.int32, (blk, 2 * blk), 0)
    ci = lax.broadcasted_iota(jnp.int32, (blk, 2 * blk), 1)
    dist = qi + blk - ci
    valid = (dist >= 0) & (dist <= blk)
    neg_inf = F32(-jnp.inf)
    neg_dist = jnp.where(valid, -(dist * d).astype(F32), neg_inf)
    has_prev = (lt > 0) | (ci >= blk)
    lane = lax.broadcasted_iota(jnp.int32, (blk, LANES), 1)
    scale = HEAD_DIM ** -0.5

    for r in range(d):
        for i in range(tl // blk):
            if d == 1:
                rows = pl.ds(i * blk, blk)
            else:
                rows = pl.ds(i * blk * d + r, blk, stride=d)
            for h in range(heads):
                if h % LSE_HEADS == 0:
                    lse_tile = jnp.zeros((blk, LANES), F32)
                hc = slice(h * HEAD_DIM, (h + 1) * HEAD_DIM)
                q = q_ref[r, i * blk:(i + 1) * blk, hc]
                if i == 0:
                    kk = jnp.concatenate([kp_ref[r, :, hc], kc_ref[r, 0:blk, hc]], axis=0)
                    vv = jnp.concatenate([vp_ref[r, :, hc], vc_ref[r, 0:blk, hc]], axis=0)
                else:
                    kk = kc_ref[r, (i - 1) * blk:(i + 1) * blk, hc]
                    vv = vc_ref[r, (i - 1) * blk:(i + 1) * blk, hc]
                s = lax.dot_general(q, kk, (((1,), (1,)), ((), ())), preferred_element_type=F32)
                s = s * scale + slopes_ref[head0 + h] * neg_dist
                if i == 0:
                    s = jnp.where(has_prev, s, neg_inf)
                m = jnp.max(s, axis=-1, keepdims=True)
                p = jnp.exp(s - m)
                l = jnp.sum(p, axis=-1, keepdims=True)
                o = jnp.dot(p.astype(BF16), vv, preferred_element_type=F32)
                o_ref[h, rows, :] = o * (1.0 / l)
                lse_tile = jnp.where(lane == h % LSE_HEADS, m + jnp.log(l), lse_tile)
                if h % LSE_HEADS == LSE_HEADS - 1:
                    lse_ref[h // LSE_HEADS, rows, :] = lse_tile


def _attention_group(slopes, q, k, v, batch, seq, group):
    d = DILATIONS[group]
    assert WINDOWS[group] // d == ATTN_BLOCK
    sub = seq // d
    tl = ATTN_TL[group]
    blk = ATTN_BLOCK
    heads = ATTN_HEADS[group]
    hw = heads * HEAD_DIM
    span = tl * d
    shape = (batch, d, sub, A_WIDTH)
    qv, kv, vv = q.reshape(shape), k.reshape(shape), v.reshape(shape)
    cur = pl.BlockSpec((None, d, tl, hw), lambda b, lt, hx: (b, 0, lt, hx))
    prev = pl.BlockSpec((None, d, blk, hw),
                        lambda b, lt, hx: (b, 0, jnp.maximum(lt * (tl // blk) - 1, 0), hx))
    o, lse = pl.pallas_call(
        _attn_kernel,
        out_shape=[jax.ShapeDtypeStruct((N_HEADS, batch, seq, HEAD_DIM), F32),
                   jax.ShapeDtypeStruct((N_HEADS // LSE_HEADS, batch, seq, LANES), F32)],
        grid=(batch, sub // tl, N_HEADS // heads),
        in_specs=[pl.BlockSpec(memory_space=pltpu.SMEM), cur, cur, prev, cur, prev],
        out_specs=[pl.BlockSpec((heads, None, span, HEAD_DIM), lambda b, lt, hx: (hx, b, lt, 0)),
                   pl.BlockSpec((heads // LSE_HEADS, None, span, LANES), lambda b, lt, hx: (hx, b, lt, 0))],
        compiler_params=_params(("parallel", "parallel", "parallel")),
        name=f"attn_d{d}",
    )(slopes, qv, kv, kv, vv, vv)
    n = batch * seq
    return o.reshape(N_HEADS, n, HEAD_DIM), lse.reshape(N_HEADS // LSE_HEADS, n, LANES)


def _even_out_kernel(x_ref, o0_ref, o1_ref, o2_ref, l0_ref, l1_ref, l2_ref, yb_ref, w_ref, out_ref):
    lses = [l0_ref[...], l1_ref[...], l2_ref[...]]
    mx = jnp.maximum(jnp.maximum(lses[0], lses[1]), lses[2])
    es = [jnp.exp(l - mx) for l in lses]
    inv = 1.0 / (es[0] + es[1] + es[2])
    alphas = [e * inv for e in es]
    o_refs = (o0_ref, o1_ref, o2_ref)
    parts = []
    for h in range(N_HEADS):
        chunk, lane = divmod(h, LSE_HEADS)
        y = alphas[0][chunk][:, lane:lane + 1] * o_refs[0][h]
        for g in (1, 2):
            y = y + alphas[g][chunk][:, lane:lane + 1] * o_refs[g][h]
        parts.append(y.astype(BF16))
    ya = jnp.concatenate(parts, axis=-1)
    acc = jnp.dot(ya, w_ref[0:A_WIDTH, :], preferred_element_type=F32)
    acc = acc + jnp.dot(yb_ref[...], w_ref[A_WIDTH:, :], preferred_element_type=F32)
    out_ref[...] = x_ref[...] + acc


def _even_out(x, outs, lses, yb, w_out, layer_idx):
    n, d = x.shape
    tm = PROJ_TM
    nchunk = N_HEADS // LSE_HEADS
    row = lambda i: (i, 0)
    return pl.pallas_call(
        _even_out_kernel,
        out_shape=jax.ShapeDtypeStruct((n, d), F32),
        grid=(n // tm,),
        in_specs=[pl.BlockSpec((tm, d), row)]
        + [pl.BlockSpec((N_HEADS, tm, HEAD_DIM), lambda i: (0, i, 0))] * 3
        + [pl.BlockSpec((nchunk, tm, LANES), lambda i: (0, i, 0))] * 3
        + [pl.BlockSpec((tm, d - A_WIDTH), row),
           _resident((None, d, d), lambda i: (layer_idx, 0, 0))],
        out_specs=pl.BlockSpec((tm, d), row),
        compiler_params=_params(("parallel",)),
        name="even_out",
    )(x, *outs, *lses, yb, w_out)


def _odd_mixer_kernel(x_ref, w_ref, cw_ref, cb_ref, lg_ref, lb_ref, pw_ref, ps_ref,
                      u_ref, yd_ref, ubuf, zbuf, cbuf, *, tiles_per_seq, conv_taps):
    tm = x_ref.shape[0]
    cw_width = u_ref.shape[-1]
    dw_width = yd_ref.shape[-1]
    i = pl.program_id(0)
    first = (i % tiles_per_seq) == 0
    x = x_ref[...]
    _carry_history(ubuf, CONV_HALO, tm, first)
    _carry_history(zbuf, POOL_HALO, tm, first)

    for c in range(cw_width // PROJ_TN):
        cols = slice(c * PROJ_TN, (c + 1) * PROJ_TN)
        a = jnp.dot(x, w_ref[:, c * PROJ_TN:(c + 1) * PROJ_TN], preferred_element_type=F32)
        gate = jnp.dot(x, w_ref[:, cw_width + c * PROJ_TN:cw_width + (c + 1) * PROJ_TN],
                       preferred_element_type=F32)
        ubuf[CONV_HALO:CONV_HALO + tm, cols] = a * jax.nn.sigmoid(gate)
    for c in range(dw_width // PROJ_TN):
        cols = slice(c * PROJ_TN, (c + 1) * PROJ_TN)
        lo = 2 * cw_width + c * PROJ_TN
        zbuf[POOL_HALO:POOL_HALO + tm, cols] = jnp.dot(x, w_ref[:, lo:lo + PROJ_TN],
                                                       preferred_element_type=F32)

    lead = CONV_HALO - (conv_taps - 1)
    rc = CONV_ROWS
    for r0 in range(0, tm, rc):
        for c0 in range(0, cw_width, LANES):
            y = None
            for b in range(SUBLANES):
                zb = None
                for a in range(-(-(conv_taps + lead) // SUBLANES)):
                    j = SUBLANES * a + b - lead
                    if 0 <= j < conv_taps:
                        lo = r0 + SUBLANES * a
                        term = (cw_ref[j:j + 1, c0:c0 + LANES]
                                * ubuf[lo:lo + rc + (SUBLANES if b else 0), c0:c0 + LANES])
                        zb = term if zb is None else zb + term
                zs = zb[b:b + rc] if b else zb
                y = zs if y is None else y + zs
            cbuf[r0:r0 + rc, c0:c0 + LANES] = y + cb_ref[:, c0:c0 + LANES]

    cv = cbuf[...]
    mu = jnp.mean(cv, axis=-1, keepdims=True)
    cen = cv - mu
    var = jnp.mean(cen * cen, axis=-1, keepdims=True)
    yn = cen * lax.rsqrt(var + EPS) * lg_ref[...] + lb_ref[...]
    u_ref[...] = (yn * jax.nn.sigmoid(yn)).astype(BF16)

    t_in_seq = (i % tiles_per_seq) * tm + lax.broadcasted_iota(jnp.int32, (tm, 1), 0)
    t1 = (t_in_seq + 1).astype(F32)
    gw = dw_width // len(POOL_WINDOWS)
    for g, kw in enumerate(POOL_WINDOWS):
        cols = slice(g * gw, (g + 1) * gw)
        z = zbuf[POOL_HALO:POOL_HALO + tm, cols]
        tot = z
        for k in range(1, kw):
            tot = tot + zbuf[POOL_HALO - k:POOL_HALO - k + tm, cols]
        pooled = tot * (1.0 / jnp.minimum(t1, float(kw))) - z
        y = jnp.dot(pooled.astype(BF16), pw_ref[g], preferred_element_type=F32)
        yd_ref[:, cols] = (y * ps_ref[:, cols]).astype(BF16)


def _odd_mixer(xn, w_in, conv_w, conv_b, ln_g, ln_b, pool_w, pool_scale, layer_idx, seq):
    n, d = xn.shape
    cw_width = conv_w.shape[-1]
    dw_width = pool_scale.shape[-1]
    taps = conv_w.shape[-2]
    assert taps - 1 <= CONV_HALO and max(POOL_WINDOWS) - 1 <= POOL_HALO
    tm = PROJ_TM
    ngroups, gw = pool_w.shape[1], pool_w.shape[2]
    vec = lambda width: pl.BlockSpec((None, 1, width), lambda i: (layer_idx, 0, 0))
    return pl.pallas_call(
        functools.partial(_odd_mixer_kernel, tiles_per_seq=seq // tm, conv_taps=taps),
        out_shape=[jax.ShapeDtypeStruct((n, cw_width), BF16),
                   jax.ShapeDtypeStruct((n, dw_width), BF16)],
        grid=(n // tm,),
        in_specs=[
            pl.BlockSpec((tm, d), lambda i: (i, 0)),
            _resident((None, d, 2 * cw_width + dw_width), lambda i: (layer_idx, 0, 0)),
            pl.BlockSpec((None, taps, cw_width), lambda i: (layer_idx, 0, 0)),
            vec(cw_width), vec(cw_width), vec(cw_width),
            pl.BlockSpec((None, ngroups, gw, gw), lambda i: (layer_idx, 0, 0, 0)),
            vec(dw_width),
        ],
        out_specs=[pl.BlockSpec((tm, cw_width), lambda i: (i, 0)),
                   pl.BlockSpec((tm, dw_width), lambda i: (i, 0))],
        scratch_shapes=[pltpu.VMEM((CONV_HALO + tm, cw_width), F32),
                        pltpu.VMEM((POOL_HALO + tm, dw_width), F32),
                        pltpu.VMEM((tm, cw_width), F32)],
        compiler_params=_params(("arbitrary",)),
        name="odd_mixer",
    )(xn, w_in, conv_w, conv_b, ln_g, ln_b, pool_w, pool_scale)


def _odd_out_kernel(x_ref, ya_ref, yb_ref, w_ref, out_ref):
    half = ya_ref.shape[-1]
    acc = jnp.dot(ya_ref[...], w_ref[0:half, :], preferred_element_type=F32)
    acc = acc + jnp.dot(yb_ref[...], w_ref[half:, :], preferred_element_type=F32)
    out_ref[...] = x_ref[...] + acc


def _odd_out(x, ya, yb, w_out, layer_idx):
    n, d = x.shape
    tm = PROJ_TM
    row = lambda i: (i, 0)
    return pl.pallas_call(
        _odd_out_kernel,
        out_shape=jax.ShapeDtypeStruct((n, d), F32),
        grid=(n // tm,),
        in_specs=[pl.BlockSpec((tm, d), row),
                  pl.BlockSpec((tm, ya.shape[-1]), row),
                  pl.BlockSpec((tm, yb.shape[-1]), row),
                  _resident((None, d, d), lambda i: (layer_idx, 0, 0))],
        out_specs=pl.BlockSpec((tm, d), row),
        compiler_params=_params(("parallel",)),
        name="odd_out",
    )(x, ya, yb, w_out)


def _by_residue(xn, batch, seq, dilation):
    if dilation == 1:
        return xn
    d = xn.shape[-1]
    x4 = xn.reshape(batch, seq // dilation, dilation, d)
    return jnp.swapaxes(x4, 1, 2).reshape(batch * seq, d)


def kernel(x, norm_g, ffn_w1, ffn_w3, ffn_w2, ev_w_in, ev_q_gain, ev_k_gain, ev_conv_w, ev_w_out,
           od_w_in, od_conv_w, od_conv_b, od_ln_g, od_ln_b, od_pool_w, od_pool_scale, od_w_out):
    batch, seq, d = x.shape
    depth = norm_g.shape[0]
    n = batch * seq
    f = ffn_w1.shape[-1]
    assert d - A_WIDTH == ev_conv_w.shape[-1] and seq % (DILATIONS[-1] * ATTN_BLOCK) == 0
    assert n % FFN_TM == 0 and seq % PROJ_TM == 0 and f % FFN_TF == 0 and f // FFN_TF >= 2
    assert all(hd % LSE_HEADS == 0 and N_HEADS % hd == 0 for hd in ATTN_HEADS)
    assert all((seq // dl) % tl == 0 for dl, tl in zip(DILATIONS, ATTN_TL))

    w1, w3, w2 = (w.astype(BF16) for w in (ffn_w1, ffn_w3, ffn_w2))
    ev_in, ev_out, od_in, od_out = (w.astype(BF16) for w in (ev_w_in, ev_w_out, od_w_in, od_w_out))
    pool_w = od_pool_w.astype(BF16)
    gains = norm_g[:, :, None, :]
    row = lambda v: v[:, None, :]
    q_gain, k_gain = row(ev_q_gain), row(ev_k_gain)
    conv_b, ln_g, ln_b, pool_scale = row(od_conv_b), row(od_ln_g), row(od_ln_b), row(od_pool_scale)
    slopes = jnp.asarray(2.0 ** (-8.0 * np.arange(1, N_HEADS + 1) / N_HEADS), dtype=F32)

    h = x.reshape(n, d)
    for layer in range(depth):
        idx = layer // 2
        h, xn = _ffn(h, gains, w1, w3, w2, layer, 0, 0, next_gain_idx=1)
        if layer % 2 == 0:
            outs, lses = [], []
            for g, dl in enumerate(DILATIONS):
                q, k, v = _group_proj(_by_residue(xn, batch, seq, dl), ev_in, q_gain, k_gain, idx, g)
                o, lse = _attention_group(slopes, q, k, v, batch, seq, g)
                outs.append(o)
                lses.append(lse)
            yb = _short_conv(xn, ev_in, ev_conv_w, idx, seq)
            h = _even_out(h, outs, lses, yb, ev_out, idx)
        else:
            u, yd = _odd_mixer(xn, od_in, od_conv_w, conv_b, ln_g, ln_b, pool_w, pool_scale, idx, seq)
            h = _odd_out(h, u, yd, od_out, idx)
        h = _ffn(h, gains, w1, w3, w2, layer, 1, 2)
    return h.reshape(batch, seq, d)
```

```python
import functools

import numpy as np
import jax
import jax.numpy as jnp
from jax import lax
from jax.experimental import pallas as pl
from jax.experimental.pallas import tpu as pltpu

F32 = jnp.float32
BF16 = jnp.bfloat16
EPS = 1e-6

HEAD_DIM = 128
N_HEADS = 8
A_WIDTH = N_HEADS * HEAD_DIM
WINDOWS = (128, 512, 2048)
DILATIONS = (1, 4, 16)
ATTN_BLOCK = 128
POOL_WINDOWS = (2, 4, 8, 16)

LANES = 128
SUBLANES = 8
VMEM_LIMIT = 60000 * 1024
FFN_TM = 1024
FFN_TF = 512
PROJ_TM = 512
PROJ_TN = 512
ATTN_PLANES = (1, 16, 16)
ATTN_TL = (512, 64, 128)
ATTN_HEADS = (8, 8, 4)
LSE_HEADS = 4
CONV_HALO = 32
CONV_ROWS = 128
POOL_HALO = 16


def _params(semantics):
    return pltpu.CompilerParams(dimension_semantics=semantics, vmem_limit_bytes=VMEM_LIMIT)


def _resident(block_shape, index_map):
    return pl.BlockSpec(block_shape, index_map, pipeline_mode=pl.Buffered(1))


def _rms(x, gain):
    ms = jnp.mean(x * x, axis=-1, keepdims=True)
    return x * lax.rsqrt(ms + EPS) * gain


def _ffn_kernel(*refs, with_next):
    if with_next:
        x_hbm, g_ref, w1_ref, w3_ref, w2_ref, gn_ref, o_ref, xn_ref, xs_ref, xbuf, xsem = refs
    else:
        x_hbm, g_ref, w1_ref, w3_ref, w2_ref, o_ref, xs_ref, xbuf, xsem = refs
    i, j = pl.program_id(0), pl.program_id(1)
    tm = xbuf.shape[0]

    def x_copy(tile):
        return pltpu.make_async_copy(x_hbm.at[pl.ds(tile * tm, tm), :], xbuf, xsem)

    @pl.when((i == 0) & (j == 0))
    def _():
        x_copy(0).start()

    @pl.when(j == 0)
    def _():
        x_copy(i).wait()
        xf = xbuf[...]
        xs_ref[...] = _rms(xf, g_ref[...]).astype(BF16)
        o_ref[...] = xf

    @pl.when((j == 1) & (i + 1 < pl.num_programs(0)))
    def _():
        x_copy(i + 1).start()

    xn = xs_ref[...]
    h1 = jnp.dot(xn, w1_ref[...], preferred_element_type=F32)
    h3 = jnp.dot(xn, w3_ref[...], preferred_element_type=F32)
    a = (0.5 * (h1 * jax.nn.sigmoid(h1)) * h3).astype(BF16)
    o_ref[...] += jnp.dot(a, w2_ref[...], preferred_element_type=F32)

    if with_next:
        @pl.when(j == pl.num_programs(1) - 1)
        def _():
            xn_ref[...] = _rms(o_ref[...], gn_ref[...]).astype(BF16)


def _ffn(x, norm_g, w1, w3, w2, layer, which, gain_idx, next_gain_idx=None):
    n, d = x.shape
    tm, tf = FFN_TM, FFN_TF
    nf = w1.shape[-1] // tf
    with_next = next_gain_idx is not None
    in_specs = [
        pl.BlockSpec(memory_space=pl.ANY),
        pl.BlockSpec((None, None, 1, d), lambda i, j: (layer, gain_idx, 0, 0)),
        pl.BlockSpec((None, None, d, tf), lambda i, j: (layer, which, 0, j)),
        pl.BlockSpec((None, None, d, tf), lambda i, j: (layer, which, 0, j)),
        pl.BlockSpec((None, None, tf, d), lambda i, j: (layer, which, j, 0)),
    ]
    args = [x, norm_g, w1, w3, w2]
    out_shape = [jax.ShapeDtypeStruct((n, d), F32)]
    out_specs = [pl.BlockSpec((tm, d), lambda i, j: (i, 0))]
    if with_next:
        in_specs.append(pl.BlockSpec((None, None, 1, d), lambda i, j: (layer, next_gain_idx, 0, 0)))
        args.append(norm_g)
        out_shape.append(jax.ShapeDtypeStruct((n, d), BF16))
        out_specs.append(pl.BlockSpec((tm, d), lambda i, j: (i, 0)))
    res = pl.pallas_call(
        functools.partial(_ffn_kernel, with_next=with_next),
        out_shape=out_shape,
        grid=(n // tm, nf),
        in_specs=in_specs,
        out_specs=out_specs,
        scratch_shapes=[pltpu.VMEM((tm, d), BF16), pltpu.VMEM((tm, d), F32), pltpu.SemaphoreType.DMA(())],
        compiler_params=_params(("arbitrary", "arbitrary")),
        name="ffn_next" if with_next else "ffn",
    )(*args)
    return res if with_next else res[0]


def _group_proj_kernel(x_ref, wq_ref, wk_ref, wv_ref, qg_ref, kg_ref, q_out, k_out, v_out):
    x = x_ref[...]
    for w_ref, gain_ref, o_ref in ((wq_ref, qg_ref, q_out), (wk_ref, kg_ref, k_out), (wv_ref, None, v_out)):
        for c in range(A_WIDTH // PROJ_TN):
            h = jnp.dot(x, w_ref[:, c * PROJ_TN:(c + 1) * PROJ_TN], preferred_element_type=F32)
            for hh in range(PROJ_TN // HEAD_DIM):
                lo = c * PROJ_TN + hh * HEAD_DIM
                hs = h[:, hh * HEAD_DIM:(hh + 1) * HEAD_DIM]
                if gain_ref is not None:
                    hs = _rms(hs, gain_ref[...])
                o_ref[:, lo:lo + HEAD_DIM] = hs.astype(BF16)


def _group_proj(xg, w_in, q_gain, k_gain, layer_idx, group):
    n, d = xg.shape
    tm = PROJ_TM
    ngrp = len(DILATIONS)
    slab = lambda which: _resident((None, d, A_WIDTH), lambda i: (layer_idx, 0, which * ngrp + group))
    gain = pl.BlockSpec((None, 1, HEAD_DIM), lambda i: (layer_idx, 0, 0))
    out = pl.BlockSpec((tm, A_WIDTH), lambda i: (i, 0))
    return pl.pallas_call(
        _group_proj_kernel,
        out_shape=[jax.ShapeDtypeStruct((n, A_WIDTH), BF16)] * 3,
        grid=(n // tm,),
        in_specs=[pl.BlockSpec((tm, d), lambda i: (i, 0)), slab(0), slab(1), slab(2), gain, gain],
        out_specs=[out, out, out],
        compiler_params=_params(("parallel",)),
        name="group_proj",
    )(xg, w_in, w_in, w_in, q_gain, k_gain)


def _carry_history(buf, halo, tm, first):
    @pl.when(first)
    def _():
        buf[0:halo, :] = jnp.zeros((halo, buf.shape[-1]), buf.dtype)

    @pl.when(jnp.logical_not(first))
    def _():
        buf[0:halo, :] = buf[tm:tm + halo, :]


def _short_conv_kernel(x_ref, w_ref, cw_ref, o_ref, work_ref, *, tiles_per_seq):
    tm = x_ref.shape[0]
    width = o_ref.shape[-1]
    _carry_history(work_ref, SUBLANES, tm, (pl.program_id(0) % tiles_per_seq) == 0)
    x = x_ref[...]
    for c in range(width // PROJ_TN):
        cols = slice(c * PROJ_TN, (c + 1) * PROJ_TN)

        def proj(k):
            lo = k * width + c * PROJ_TN
            return jnp.dot(x, w_ref[:, lo:lo + PROJ_TN], preferred_element_type=F32)

        bg, cg, xt = proj(0), proj(1), proj(2)
        u = cg * xt
        work_ref[SUBLANES:SUBLANES + tm, cols] = u
        y = (cw_ref[0:1, cols] * work_ref[SUBLANES - 2:SUBLANES - 2 + tm, cols]
             + cw_ref[1:2, cols] * work_ref[SUBLANES - 1:SUBLANES - 1 + tm, cols]
             + cw_ref[2:3, cols] * u)
        o_ref[:, cols] = (bg * y).astype(BF16)


def _short_conv(xn, w_in, conv_w, layer_idx, seq):
    n, d = xn.shape
    width = d - A_WIDTH
    tm = PROJ_TM
    return pl.pallas_call(
        functools.partial(_short_conv_kernel, tiles_per_seq=seq // tm),
        out_shape=jax.ShapeDtypeStruct((n, width), BF16),
        grid=(n // tm,),
        in_specs=[
            pl.BlockSpec((tm, d), lambda i: (i, 0)),
            _resident((None, d, 3 * width), lambda i: (layer_idx, 0, 3)),
            pl.BlockSpec((None, 3, width), lambda i: (layer_idx, 0, 0)),
        ],
        out_specs=pl.BlockSpec((tm, width), lambda i: (i, 0)),
        scratch_shapes=[pltpu.VMEM((SUBLANES + tm, width), F32)],
        compiler_params=_params(("arbitrary",)),
        name="short_conv",
    )(xn, w_in, conv_w)


def _attn_kernel(slopes_ref, q_ref, kc_ref, kp_ref, vc_ref, vp_ref, o_ref, lse_ref, *, dilation):
    planes, tlp, hw = q_ref.shape
    blk = ATTN_BLOCK
    nchunks = planes // dilation
    chunk = blk // nchunks
    heads = hw // HEAD_DIM
    lt = pl.program_id(1)
    head0 = pl.program_id(2) * heads

    def pos(idx):
        return (idx % chunk) * nchunks + idx // chunk

    qi = lax.broadcasted_iota(jnp.int32, (blk, 2 * blk), 0)
    ci = lax.broadcasted_iota(jnp.int32, (blk, 2 * blk), 1)
    dist = pos(qi) - pos(ci % blk) + jnp.where(ci < blk, blk, 0)
    valid = (dist >= 0) & (dist <= blk)
    neg_inf = F32(-jnp.inf)
    neg_dist = jnp.where(valid, -(dist * dilation).astype(F32), neg_inf)
    has_prev = (lt > 0) | (ci >= blk)
    lane = lax.broadcasted_iota(jnp.int32, (blk, LANES), 1)
    scale = HEAD_DIM ** -0.5

    def gather(cur_ref, prev_ref, members, i, hc):
        now = [cur_ref[p, i * chunk:(i + 1) * chunk, hc] for p in members]
        if i == 0:
            before = [prev_ref[p, :, hc] for p in members]
        else:
            before = [cur_ref[p, (i - 1) * chunk:i * chunk, hc] for p in members]
        return jnp.concatenate(before + now, axis=0)

    for r in range(dilation):
        members = [r + dilation * c for c in range(nchunks)]
        for i in range(tlp // chunk):
            if planes == 1:
                dests = [pl.ds(i * chunk, chunk)]
            else:
                dests = [pl.ds(i * chunk * planes + p, chunk, stride=planes) for p in members]
            for h in range(heads):
                if h % LSE_HEADS == 0:
                    lse_tile = jnp.zeros((blk, LANES), F32)
                hc = slice(h * HEAD_DIM, (h + 1) * HEAD_DIM)
                q = jnp.concatenate([q_ref[p, i * chunk:(i + 1) * chunk, hc] for p in members], axis=0)
                kk = gather(kc_ref, kp_ref, members, i, hc)
                vv = gather(vc_ref, vp_ref, members, i, hc)
                s = lax.dot_general(q, kk, (((1,), (1,)), ((), ())), preferred_element_type=F32)
                s = s * scale + slopes_ref[head0 + h] * neg_dist
                if i == 0:
                    s = jnp.where(has_prev, s, neg_inf)
                m = jnp.max(s, axis=-1, keepdims=True)
                p = jnp.exp(s - m)
                l = jnp.sum(p, axis=-1, keepdims=True)
                o = jnp.dot(p.astype(BF16), vv, preferred_element_type=F32) * (1.0 / l)
                lse_tile = jnp.where(lane == h % LSE_HEADS, m + jnp.log(l), lse_tile)
                for c, dest in enumerate(dests):
                    o_ref[h, dest, :] = o[c * chunk:(c + 1) * chunk]
                    if h % LSE_HEADS == LSE_HEADS - 1:
                        lse_ref[h // LSE_HEADS, dest, :] = lse_tile[c * chunk:(c + 1) * chunk]


def _attention_group(slopes, q, k, v, batch, seq, group):
    d = DILATIONS[group]
    planes = ATTN_PLANES[group]
    assert WINDOWS[group] // d == ATTN_BLOCK and planes % d == 0
    rows = seq // planes
    tl = ATTN_TL[group]
    chunk = ATTN_BLOCK * d // planes
    heads = ATTN_HEADS[group]
    hw = heads * HEAD_DIM
    span = tl * planes
    shape = (batch, planes, rows, A_WIDTH)
    qv, kv, vv = q.reshape(shape), k.reshape(shape), v.reshape(shape)
    cur = pl.BlockSpec((None, planes, tl, hw), lambda b, lt, hx: (b, 0, lt, hx))
    prev = pl.BlockSpec((None, planes, chunk, hw),
                        lambda b, lt, hx: (b, 0, jnp.maximum(lt * (tl // chunk) - 1, 0), hx))
    o, lse = pl.pallas_call(
        functools.partial(_attn_kernel, dilation=d),
        out_shape=[jax.ShapeDtypeStruct((N_HEADS, batch, seq, HEAD_DIM), F32),
                   jax.ShapeDtypeStruct((N_HEADS // LSE_HEADS, batch, seq, LANES), F32)],
        grid=(batch, rows // tl, N_HEADS // heads),
        in_specs=[pl.BlockSpec(memory_space=pltpu.SMEM), cur, cur, prev, cur, prev],
        out_specs=[pl.BlockSpec((heads, None, span, HEAD_DIM), lambda b, lt, hx: (hx, b, lt, 0)),
                   pl.BlockSpec((heads // LSE_HEADS, None, span, LANES), lambda b, lt, hx: (hx, b, lt, 0))],
        compiler_params=_params(("parallel", "parallel", "parallel")),
        name=f"attn_d{d}",
    )(slopes, qv, kv, kv, vv, vv)
    n = batch * seq
    return o.reshape(N_HEADS, n, HEAD_DIM), lse.reshape(N_HEADS // LSE_HEADS, n, LANES)


def _even_out_kernel(x_ref, o0_ref, o1_ref, o2_ref, l0_ref, l1_ref, l2_ref, yb_ref, w_ref, out_ref):
    lses = [l0_ref[...], l1_ref[...], l2_ref[...]]
    mx = jnp.maximum(jnp.maximum(lses[0], lses[1]), lses[2])
    es = [jnp.exp(l - mx) for l in lses]
    inv = 1.0 / (es[0] + es[1] + es[2])
    alphas = [e * inv for e in es]
    o_refs = (o0_ref, o1_ref, o2_ref)
    parts = []
    for h in range(N_HEADS):
        chunk, lane = divmod(h, LSE_HEADS)
        y = alphas[0][chunk][:, lane:lane + 1] * o_refs[0][h]
        for g in (1, 2):
            y = y + alphas[g][chunk][:, lane:lane + 1] * o_refs[g][h]
        parts.append(y.astype(BF16))
    ya = jnp.concatenate(parts, axis=-1)
    acc = jnp.dot(ya, w_ref[0:A_WIDTH, :], preferred_element_type=F32)
    acc = acc + jnp.dot(yb_ref[...], w_ref[A_WIDTH:, :], preferred_element_type=F32)
    out_ref[...] = x_ref[...] + acc


def _even_out(x, outs, lses, yb, w_out, layer_idx):
    n, d = x.shape
    tm = PROJ_TM
    nchunk = N_HEADS // LSE_HEADS
    row = lambda i: (i, 0)
    return pl.pallas_call(
        _even_out_kernel,
        out_shape=jax.ShapeDtypeStruct((n, d), F32),
        grid=(n // tm,),
        in_specs=[pl.BlockSpec((tm, d), row)]
        + [pl.BlockSpec((N_HEADS, tm, HEAD_DIM), lambda i: (0, i, 0))] * 3
        + [pl.BlockSpec((nchunk, tm, LANES), lambda i: (0, i, 0))] * 3
        + [pl.BlockSpec((tm, d - A_WIDTH), row),
           _resident((None, d, d), lambda i: (layer_idx, 0, 0))],
        out_specs=pl.BlockSpec((tm, d), row),
        compiler_params=_params(("parallel",)),
        name="even_out",
    )(x, *outs, *lses, yb, w_out)


def _odd_mixer_kernel(x_ref, w_ref, cw_ref, cb_ref, lg_ref, lb_ref, pw_ref, ps_ref,
                      u_ref, yd_ref, ubuf, zbuf, cbuf, *, tiles_per_seq, conv_taps):
    tm = x_ref.shape[0]
    cw_width = u_ref.shape[-1]
    dw_width = yd_ref.shape[-1]
    i = pl.program_id(0)
    first = (i % tiles_per_seq) == 0
    x = x_ref[...]
    _carry_history(ubuf, CONV_HALO, tm, first)
    _carry_history(zbuf, POOL_HALO, tm, first)

    for c in range(cw_width // PROJ_TN):
        cols = slice(c * PROJ_TN, (c + 1) * PROJ_TN)
        a = jnp.dot(x, w_ref[:, c * PROJ_TN:(c + 1) * PROJ_TN], preferred_element_type=F32)
        gate = jnp.dot(x, w_ref[:, cw_width + c * PROJ_TN:cw_width + (c + 1) * PROJ_TN],
                       preferred_element_type=F32)
        ubuf[CONV_HALO:CONV_HALO + tm, cols] = a * jax.nn.sigmoid(gate)
    for c in range(dw_width // PROJ_TN):
        cols = slice(c * PROJ_TN, (c + 1) * PROJ_TN)
        lo = 2 * cw_width + c * PROJ_TN
        zbuf[POOL_HALO:POOL_HALO + tm, cols] = jnp.dot(x, w_ref[:, lo:lo + PROJ_TN],
                                                       preferred_element_type=F32)

    lead = CONV_HALO - (conv_taps - 1)
    rc = CONV_ROWS
    for r0 in range(0, tm, rc):
        for c0 in range(0, cw_width, LANES):
            y = None
            for b in range(SUBLANES):
                zb = None
                for a in range(-(-(conv_taps + lead) // SUBLANES)):
                    j = SUBLANES * a + b - lead
                    if 0 <= j < conv_taps:
                        lo = r0 + SUBLANES * a
                        term = (cw_ref[j:j + 1, c0:c0 + LANES]
                                * ubuf[lo:lo + rc + (SUBLANES if b else 0), c0:c0 + LANES])
                        zb = term if zb is None else zb + term
                zs = pltpu.roll(zb, rc + SUBLANES - b, axis=0)[0:rc] if b else zb
                y = zs if y is None else y + zs
            cbuf[r0:r0 + rc, c0:c0 + LANES] = y + cb_ref[:, c0:c0 + LANES]

    cv = cbuf[...]
    mu = jnp.mean(cv, axis=-1, keepdims=True)
    cen = cv - mu
    var = jnp.mean(cen * cen, axis=-1, keepdims=True)
    yn = cen * lax.rsqrt(var + EPS) * lg_ref[...] + lb_ref[...]
    u_ref[...] = (yn * jax.nn.sigmoid(yn)).astype(BF16)

    t_in_seq = (i % tiles_per_seq) * tm + lax.broadcasted_iota(jnp.int32, (tm, 1), 0)
    t1 = (t_in_seq + 1).astype(F32)
    gw = dw_width // len(POOL_WINDOWS)
    for g, kw in enumerate(POOL_WINDOWS):
        cols = slice(g * gw, (g + 1) * gw)
        zx = zbuf[0:POOL_HALO + tm, cols]
        tot = zx
        shift = 1
        while shift < kw:
            tot = tot + pltpu.roll(tot, shift, axis=0)
            shift *= 2
        z = zx[POOL_HALO:]
        pooled = tot[POOL_HALO:] * (1.0 / jnp.minimum(t1, float(kw))) - z
        y = jnp.dot(pooled.astype(BF16), pw_ref[g], preferred_element_type=F32)
        yd_ref[:, cols] = (y * ps_ref[:, cols]).astype(BF16)


def _odd_mixer(xn, w_in, conv_w, conv_b, ln_g, ln_b, pool_w, pool_scale, layer_idx, seq):
    n, d = xn.shape
    cw_width = conv_w.shape[-1]
    dw_width = pool_scale.shape[-1]
    taps = conv_w.shape[-2]
    assert taps - 1 <= CONV_HALO and max(POOL_WINDOWS) - 1 <= POOL_HALO
    tm = PROJ_TM
    ngroups, gw = pool_w.shape[1], pool_w.shape[2]
    vec = lambda width: pl.BlockSpec((None, 1, width), lambda i: (layer_idx, 0, 0))
    return pl.pallas_call(
        functools.partial(_odd_mixer_kernel, tiles_per_seq=seq // tm, conv_taps=taps),
        out_shape=[jax.ShapeDtypeStruct((n, cw_width), BF16),
                   jax.ShapeDtypeStruct((n, dw_width), BF16)],
        grid=(n // tm,),
        in_specs=[
            pl.BlockSpec((tm, d), lambda i: (i, 0)),
            _resident((None, d, 2 * cw_width + dw_width), lambda i: (layer_idx, 0, 0)),
            pl.BlockSpec((None, taps, cw_width), lambda i: (layer_idx, 0, 0)),
            vec(cw_width), vec(cw_width), vec(cw_width),
            pl.BlockSpec((None, ngroups, gw, gw), lambda i: (layer_idx, 0, 0, 0)),
            vec(dw_width),
        ],
        out_specs=[pl.BlockSpec((tm, cw_width), lambda i: (i, 0)),
                   pl.BlockSpec((tm, dw_width), lambda i: (i, 0))],
        scratch_shapes=[pltpu.VMEM((CONV_HALO + tm, cw_width), F32),
                        pltpu.VMEM((POOL_HALO + tm, dw_width), F32),
                        pltpu.VMEM((tm, cw_width), F32)],
        compiler_params=_params(("arbitrary",)),
        name="odd_mixer",
    )(xn, w_in, conv_w, conv_b, ln_g, ln_b, pool_w, pool_scale)


def _odd_out_kernel(x_ref, ya_ref, yb_ref, w_ref, out_ref):
    half = ya_ref.shape[-1]
    acc = jnp.dot(ya_ref[...], w_ref[0:half, :], preferred_element_type=F32)
    acc = acc + jnp.dot(yb_ref[...], w_ref[half:, :], preferred_element_type=F32)
    out_ref[...] = x_ref[...] + acc


def _odd_out(x, ya, yb, w_out, layer_idx):
    n, d = x.shape
    tm = PROJ_TM
    row = lambda i: (i, 0)
    return pl.pallas_call(
        _odd_out_kernel,
        out_shape=jax.ShapeDtypeStruct((n, d), F32),
        grid=(n // tm,),
        in_specs=[pl.BlockSpec((tm, d), row),
                  pl.BlockSpec((tm, ya.shape[-1]), row),
                  pl.BlockSpec((tm, yb.shape[-1]), row),
                  _resident((None, d, d), lambda i: (layer_idx, 0, 0))],
        out_specs=pl.BlockSpec((tm, d), row),
        compiler_params=_params(("parallel",)),
        name="odd_out",
    )(x, ya, yb, w_out)


def _by_plane(xn, batch, seq, planes):
    if planes == 1:
        return xn
    d = xn.shape[-1]
    x4 = xn.reshape(batch, seq // planes, planes, d)
    return jnp.swapaxes(x4, 1, 2).reshape(batch * seq, d)


def kernel(x, norm_g, ffn_w1, ffn_w3, ffn_w2, ev_w_in, ev_q_gain, ev_k_gain, ev_conv_w, ev_w_out,
           od_w_in, od_conv_w, od_conv_b, od_ln_g, od_ln_b, od_pool_w, od_pool_scale, od_w_out):
    batch, seq, d = x.shape
    depth = norm_g.shape[0]
    n = batch * seq
    f = ffn_w1.shape[-1]
    assert d - A_WIDTH == ev_conv_w.shape[-1] and seq % (DILATIONS[-1] * ATTN_BLOCK) == 0
    assert n % FFN_TM == 0 and seq % PROJ_TM == 0 and f % FFN_TF == 0 and f // FFN_TF >= 2
    assert all(hd % LSE_HEADS == 0 and N_HEADS % hd == 0 for hd in ATTN_HEADS)
    assert all((seq // p) % tl == 0 and tl % (ATTN_BLOCK * dl // p) == 0
               for dl, p, tl in zip(DILATIONS, ATTN_PLANES, ATTN_TL))

    w1, w3, w2 = (w.astype(BF16) for w in (ffn_w1, ffn_w3, ffn_w2))
    ev_in, ev_out, od_in, od_out = (w.astype(BF16) for w in (ev_w_in, ev_w_out, od_w_in, od_w_out))
    pool_w = od_pool_w.astype(BF16)
    gains = norm_g[:, :, None, :]
    row = lambda v: v[:, None, :]
    q_gain, k_gain = row(ev_q_gain), row(ev_k_gain)
    conv_b, ln_g, ln_b, pool_scale = row(od_conv_b), row(od_ln_g), row(od_ln_b), row(od_pool_scale)
    slopes = jnp.asarray(2.0 ** (-8.0 * np.arange(1, N_HEADS + 1) / N_HEADS), dtype=F32)

    h = x.reshape(n, d)
    for layer in range(depth):
        idx = layer // 2
        h, xn = _ffn(h, gains, w1, w3, w2, layer, 0, 0, next_gain_idx=1)
        if layer % 2 == 0:
            outs, lses = [], []
            by_plane = {p: _by_plane(xn, batch, seq, p) for p in set(ATTN_PLANES)}
            for g in range(len(DILATIONS)):
                q, k, v = _group_proj(by_plane[ATTN_PLANES[g]], ev_in, q_gain, k_gain, idx, g)
                o, lse = _attention_group(slopes, q, k, v, batch, seq, g)
                outs.append(o)
                lses.append(lse)
            yb = _short_conv(xn, ev_in, ev_conv_w, idx, seq)
            h = _even_out(h, outs, lses, yb, ev_out, idx)
        else:
            u, yd = _odd_mixer(xn, od_in, od_conv_w, conv_b, ln_g, ln_b, pool_w, pool_scale, idx, seq)
            h = _odd_out(h, u, yd, od_out, idx)
        h = _ffn(h, gains, w1, w3, w2, layer, 1, 2)
    return h.reshape(batch, seq, d)
```

```python
import functools

import numpy as np
import jax
import jax.numpy as jnp
from jax import lax
from jax.experimental import pallas as pl
from jax.experimental.pallas import tpu as pltpu

F32 = jnp.float32
BF16 = jnp.bfloat16
EPS = 1e-6

HEAD_DIM = 128
N_HEADS = 8
A_WIDTH = N_HEADS * HEAD_DIM
WINDOWS = (128, 512, 2048)
DILATIONS = (1, 4, 16)
ATTN_BLOCK = 128
POOL_WINDOWS = (2, 4, 8, 16)

LANES = 128
SUBLANES = 8
VMEM_LIMIT = 60000 * 1024
FFN_TM = 1024
FFN_TF = 512
PROJ_TM = 512
PROJ_TN = 512
ODD_TN = 256
ATTN_PLANES = (1, 16, 16)
ATTN_TL = (512, 64, 128)
ATTN_HEADS = (8, 8, 4)
LSE_HEADS = 4
CONV_HALO = 32
CONV_ROWS = 128
POOL_HALO = 16


def _params(semantics):
    return pltpu.CompilerParams(dimension_semantics=semantics, vmem_limit_bytes=VMEM_LIMIT)


def _resident(block_shape, index_map):
    return pl.BlockSpec(block_shape, index_map, pipeline_mode=pl.Buffered(1))


def _rms(x, gain):
    ms = jnp.mean(x * x, axis=-1, keepdims=True)
    return x * lax.rsqrt(ms + EPS) * gain


def _ffn_kernel(*refs, with_next):
    if with_next:
        x_hbm, g_ref, w1_ref, w3_ref, w2_ref, gn_ref, o_ref, xn_ref, xs_ref, xbuf, xsem = refs
    else:
        x_hbm, g_ref, w1_ref, w3_ref, w2_ref, o_ref, xs_ref, xbuf, xsem = refs
    i, j = pl.program_id(0), pl.program_id(1)
    tm = xbuf.shape[0]

    def x_copy(tile):
        return pltpu.make_async_copy(x_hbm.at[pl.ds(tile * tm, tm), :], xbuf, xsem)

    @pl.when((i == 0) & (j == 0))
    def _():
        x_copy(0).start()

    @pl.when(j == 0)
    def _():
        x_copy(i).wait()
        xf = xbuf[...]
        xs_ref[...] = _rms(xf, g_ref[...]).astype(BF16)
        o_ref[...] = xf

    @pl.when((j == 1) & (i + 1 < pl.num_programs(0)))
    def _():
        x_copy(i + 1).start()

    xn = xs_ref[...]
    h1 = jnp.dot(xn, w1_ref[...], preferred_element_type=F32)
    h3 = jnp.dot(xn, w3_ref[...], preferred_element_type=F32)
    a = (0.5 * (h1 * jax.nn.sigmoid(h1)) * h3).astype(BF16)
    o_ref[...] += jnp.dot(a, w2_ref[...], preferred_element_type=F32)

    if with_next:
        @pl.when(j == pl.num_programs(1) - 1)
        def _():
            xn_ref[...] = _rms(o_ref[...], gn_ref[...]).astype(BF16)


def _ffn(x, norm_g, w1, w3, w2, layer, which, gain_idx, next_gain_idx=None):
    n, d = x.shape
    tm, tf = FFN_TM, FFN_TF
    nf = w1.shape[-1] // tf
    with_next = next_gain_idx is not None
    in_specs = [
        pl.BlockSpec(memory_space=pl.ANY),
        pl.BlockSpec((None, None, 1, d), lambda i, j: (layer, gain_idx, 0, 0)),
        pl.BlockSpec((None, None, d, tf), lambda i, j: (layer, which, 0, j)),
        pl.BlockSpec((None, None, d, tf), lambda i, j: (layer, which, 0, j)),
        pl.BlockSpec((None, None, tf, d), lambda i, j: (layer, which, j, 0)),
    ]
    args = [x, norm_g, w1, w3, w2]
    out_shape = [jax.ShapeDtypeStruct((n, d), F32)]
    out_specs = [pl.BlockSpec((tm, d), lambda i, j: (i, 0))]
    if with_next:
        in_specs.append(pl.BlockSpec((None, None, 1, d), lambda i, j: (layer, next_gain_idx, 0, 0)))
        args.append(norm_g)
        out_shape.append(jax.ShapeDtypeStruct((n, d), BF16))
        out_specs.append(pl.BlockSpec((tm, d), lambda i, j: (i, 0)))
    res = pl.pallas_call(
        functools.partial(_ffn_kernel, with_next=with_next),
        out_shape=out_shape,
        grid=(n // tm, nf),
        in_specs=in_specs,
        out_specs=out_specs,
        scratch_shapes=[pltpu.VMEM((tm, d), BF16), pltpu.VMEM((tm, d), F32), pltpu.SemaphoreType.DMA(())],
        compiler_params=_params(("arbitrary", "arbitrary")),
        name="ffn_next" if with_next else "ffn",
    )(*args)
    return res if with_next else res[0]


def _group_proj_kernel(x_ref, wq_ref, wk_ref, wv_ref, qg_ref, kg_ref, q_out, k_out, v_out):
    x = x_ref[...]
    for w_ref, gain_ref, o_ref in ((wq_ref, qg_ref, q_out), (wk_ref, kg_ref, k_out), (wv_ref, None, v_out)):
        for c in range(A_WIDTH // PROJ_TN):
            h = jnp.dot(x, w_ref[:, c * PROJ_TN:(c + 1) * PROJ_TN], preferred_element_type=F32)
            for hh in range(PROJ_TN // HEAD_DIM):
                lo = c * PROJ_TN + hh * HEAD_DIM
                hs = h[:, hh * HEAD_DIM:(hh + 1) * HEAD_DIM]
                if gain_ref is not None:
                    hs = _rms(hs, gain_ref[...])
                o_ref[:, lo:lo + HEAD_DIM] = hs.astype(BF16)


def _group_proj(xg, w_in, q_gain, k_gain, layer_idx, group):
    n, d = xg.shape
    tm = PROJ_TM
    ngrp = len(DILATIONS)
    slab = lambda which: _resident((None, d, A_WIDTH), lambda i: (layer_idx, 0, which * ngrp + group))
    gain = pl.BlockSpec((None, 1, HEAD_DIM), lambda i: (layer_idx, 0, 0))
    out = pl.BlockSpec((tm, A_WIDTH), lambda i: (i, 0))
    return pl.pallas_call(
        _group_proj_kernel,
        out_shape=[jax.ShapeDtypeStruct((n, A_WIDTH), BF16)] * 3,
        grid=(n // tm,),
        in_specs=[pl.BlockSpec((tm, d), lambda i: (i, 0)), slab(0), slab(1), slab(2), gain, gain],
        out_specs=[out, out, out],
        compiler_params=_params(("parallel",)),
        name="group_proj",
    )(xg, w_in, w_in, w_in, q_gain, k_gain)


def _carry_history(buf, halo, tm, first):
    @pl.when(first)
    def _():
        buf[0:halo, :] = jnp.zeros((halo, buf.shape[-1]), buf.dtype)

    @pl.when(jnp.logical_not(first))
    def _():
        buf[0:halo, :] = buf[tm:tm + halo, :]


def _short_conv_kernel(x_ref, w_ref, cw_ref, o_ref, work_ref, *, tiles_per_seq):
    tm = x_ref.shape[0]
    width = o_ref.shape[-1]
    _carry_history(work_ref, SUBLANES, tm, (pl.program_id(0) % tiles_per_seq) == 0)
    x = x_ref[...]
    for c in range(width // PROJ_TN):
        cols = slice(c * PROJ_TN, (c + 1) * PROJ_TN)

        def proj(k):
            lo = k * width + c * PROJ_TN
            return jnp.dot(x, w_ref[:, lo:lo + PROJ_TN], preferred_element_type=F32)

        bg, cg, xt = proj(0), proj(1), proj(2)
        u = cg * xt
        work_ref[SUBLANES:SUBLANES + tm, cols] = u
        y = (cw_ref[0:1, cols] * work_ref[SUBLANES - 2:SUBLANES - 2 + tm, cols]
             + cw_ref[1:2, cols] * work_ref[SUBLANES - 1:SUBLANES - 1 + tm, cols]
             + cw_ref[2:3, cols] * u)
        o_ref[:, cols] = (bg * y).astype(BF16)


def _short_conv(xn, w_in, conv_w, layer_idx, seq):
    n, d = xn.shape
    width = d - A_WIDTH
    tm = PROJ_TM
    return pl.pallas_call(
        functools.partial(_short_conv_kernel, tiles_per_seq=seq // tm),
        out_shape=jax.ShapeDtypeStruct((n, width), BF16),
        grid=(n // tm,),
        in_specs=[
            pl.BlockSpec((tm, d), lambda i: (i, 0)),
            _resident((None, d, 3 * width), lambda i: (layer_idx, 0, 3)),
            pl.BlockSpec((None, 3, width), lambda i: (layer_idx, 0, 0)),
        ],
        out_specs=pl.BlockSpec((tm, width), lambda i: (i, 0)),
        scratch_shapes=[pltpu.VMEM((SUBLANES + tm, width), F32)],
        compiler_params=_params(("arbitrary",)),
        name="short_conv",
    )(xn, w_in, conv_w)


def _attn_kernel(slopes_ref, q_ref, kc_ref, kp_ref, vc_ref, vp_ref, o_ref, lse_ref, *, dilation):
    planes, tlp, hw = q_ref.shape
    blk = ATTN_BLOCK
    nchunks = planes // dilation
    chunk = blk // nchunks
    heads = hw // HEAD_DIM
    lt = pl.program_id(1)
    head0 = pl.program_id(2) * heads

    def pos(idx):
        return (idx % chunk) * nchunks + idx // chunk

    qi = lax.broadcasted_iota(jnp.int32, (blk, 2 * blk), 0)
    ci = lax.broadcasted_iota(jnp.int32, (blk, 2 * blk), 1)
    dist = pos(qi) - pos(ci % blk) + jnp.where(ci < blk, blk, 0)
    valid = (dist >= 0) & (dist <= blk)
    neg_inf = F32(-jnp.inf)
    neg_dist = jnp.where(valid, -(dist * dilation).astype(F32), neg_inf)
    has_prev = (lt > 0) | (ci >= blk)
    lane = lax.broadcasted_iota(jnp.int32, (blk, LANES), 1)
    scale = HEAD_DIM ** -0.5

    def gather(cur_ref, prev_ref, members, i, hc):
        now = [cur_ref[p, i * chunk:(i + 1) * chunk, hc] for p in members]
        if i == 0:
            before = [prev_ref[p, :, hc] for p in members]
        else:
            before = [cur_ref[p, (i - 1) * chunk:i * chunk, hc] for p in members]
        return jnp.concatenate(before + now, axis=0)

    for r in range(dilation):
        members = [r + dilation * c for c in range(nchunks)]
        for i in range(tlp // chunk):
            if planes == 1:
                dests = [pl.ds(i * chunk, chunk)]
            else:
                dests = [pl.ds(i * chunk * planes + p, chunk, stride=planes) for p in members]
            for h in range(heads):
                if h % LSE_HEADS == 0:
                    lse_tile = jnp.zeros((blk, LANES), F32)
                hc = slice(h * HEAD_DIM, (h + 1) * HEAD_DIM)
                q = jnp.concatenate([q_ref[p, i * chunk:(i + 1) * chunk, hc] for p in members], axis=0)
                kk = gather(kc_ref, kp_ref, members, i, hc)
                vv = gather(vc_ref, vp_ref, members, i, hc)
                s = lax.dot_general(q, kk, (((1,), (1,)), ((), ())), preferred_element_type=F32)
                s = s * scale + slopes_ref[head0 + h] * neg_dist
                if i == 0:
                    s = jnp.where(has_prev, s, neg_inf)
                m = jnp.max(s, axis=-1, keepdims=True)
                p = jnp.exp(s - m)
                l = jnp.sum(p, axis=-1, keepdims=True)
                o = jnp.dot(p.astype(BF16), vv, preferred_element_type=F32) * (1.0 / l)
                lse_tile = jnp.where(lane == h % LSE_HEADS, m + jnp.log(l), lse_tile)
                for c, dest in enumerate(dests):
                    o_ref[h, dest, :] = o[c * chunk:(c + 1) * chunk]
                    if h % LSE_HEADS == LSE_HEADS - 1:
                        lse_ref[h // LSE_HEADS, dest, :] = lse_tile[c * chunk:(c + 1) * chunk]


def _attention_group(slopes, q, k, v, batch, seq, group):
    d = DILATIONS[group]
    planes = ATTN_PLANES[group]
    assert WINDOWS[group] // d == ATTN_BLOCK and planes % d == 0
    rows = seq // planes
    tl = ATTN_TL[group]
    chunk = ATTN_BLOCK * d // planes
    heads = ATTN_HEADS[group]
    hw = heads * HEAD_DIM
    span = tl * planes
    shape = (batch, planes, rows, A_WIDTH)
    qv, kv, vv = q.reshape(shape), k.reshape(shape), v.reshape(shape)
    cur = pl.BlockSpec((None, planes, tl, hw), lambda b, lt, hx: (b, 0, lt, hx))
    prev = pl.BlockSpec((None, planes, chunk, hw),
                        lambda b, lt, hx: (b, 0, jnp.maximum(lt * (tl // chunk) - 1, 0), hx))
    o, lse = pl.pallas_call(
        functools.partial(_attn_kernel, dilation=d),
        out_shape=[jax.ShapeDtypeStruct((N_HEADS, batch, seq, HEAD_DIM), F32),
                   jax.ShapeDtypeStruct((N_HEADS // LSE_HEADS, batch, seq, LANES), F32)],
        grid=(batch, rows // tl, N_HEADS // heads),
        in_specs=[pl.BlockSpec(memory_space=pltpu.SMEM), cur, cur, prev, cur, prev],
        out_specs=[pl.BlockSpec((heads, None, span, HEAD_DIM), lambda b, lt, hx: (hx, b, lt, 0)),
                   pl.BlockSpec((heads // LSE_HEADS, None, span, LANES), lambda b, lt, hx: (hx, b, lt, 0))],
        compiler_params=_params(("parallel", "parallel", "parallel")),
        name=f"attn_d{d}",
    )(slopes, qv, kv, kv, vv, vv)
    n = batch * seq
    return o.reshape(N_HEADS, n, HEAD_DIM), lse.reshape(N_HEADS // LSE_HEADS, n, LANES)


def _even_out_kernel(x_ref, o0_ref, o1_ref, o2_ref, l0_ref, l1_ref, l2_ref, yb_ref, w_ref, out_ref):
    lses = [l0_ref[...], l1_ref[...], l2_ref[...]]
    mx = jnp.maximum(jnp.maximum(lses[0], lses[1]), lses[2])
    es = [jnp.exp(l - mx) for l in lses]
    inv = 1.0 / (es[0] + es[1] + es[2])
    alphas = [e * inv for e in es]
    o_refs = (o0_ref, o1_ref, o2_ref)
    parts = []
    for h in range(N_HEADS):
        chunk, lane = divmod(h, LSE_HEADS)
        y = alphas[0][chunk][:, lane:lane + 1] * o_refs[0][h]
        for g in (1, 2):
            y = y + alphas[g][chunk][:, lane:lane + 1] * o_refs[g][h]
        parts.append(y.astype(BF16))
    ya = jnp.concatenate(parts, axis=-1)
    acc = jnp.dot(ya, w_ref[0:A_WIDTH, :], preferred_element_type=F32)
    acc = acc + jnp.dot(yb_ref[...], w_ref[A_WIDTH:, :], preferred_element_type=F32)
    out_ref[...] = x_ref[...] + acc


def _even_out(x, outs, lses, yb, w_out, layer_idx):
    n, d = x.shape
    tm = PROJ_TM
    nchunk = N_HEADS // LSE_HEADS
    row = lambda i: (i, 0)
    return pl.pallas_call(
        _even_out_kernel,
        out_shape=jax.ShapeDtypeStruct((n, d), F32),
        grid=(n // tm,),
        in_specs=[pl.BlockSpec((tm, d), row)]
        + [pl.BlockSpec((N_HEADS, tm, HEAD_DIM), lambda i: (0, i, 0))] * 3
        + [pl.BlockSpec((nchunk, tm, LANES), lambda i: (0, i, 0))] * 3
        + [pl.BlockSpec((tm, d - A_WIDTH), row),
           _resident((None, d, d), lambda i: (layer_idx, 0, 0))],
        out_specs=pl.BlockSpec((tm, d), row),
        compiler_params=_params(("parallel",)),
        name="even_out",
    )(x, *outs, *lses, yb, w_out)


def _odd_mixer_kernel(x_ref, w_ref, cw_ref, cb_ref, lg_ref, lb_ref, pw_ref, ps_ref,
                      u_ref, yd_ref, ubuf, zbuf, cbuf, *, tiles_per_seq, conv_taps):
    tm = x_ref.shape[0]
    cw_width = u_ref.shape[-1]
    dw_width = yd_ref.shape[-1]
    i = pl.program_id(0)
    first = (i % tiles_per_seq) == 0
    x = x_ref[...]
    _carry_history(ubuf, CONV_HALO, tm, first)
    _carry_history(zbuf, POOL_HALO, tm, first)

    lead = CONV_HALO - (conv_taps - 1)
    rc = CONV_ROWS

    def conv_columns(lo_col, hi_col):
        for r0 in range(0, tm, rc):
            for c0 in range(lo_col, hi_col, LANES):
                y = None
                for b in range(SUBLANES):
                    zb = None
                    for a in range(-(-(conv_taps + lead) // SUBLANES)):
                        j = SUBLANES * a + b - lead
                        if 0 <= j < conv_taps:
                            lo = r0 + SUBLANES * a
                            term = (cw_ref[j:j + 1, c0:c0 + LANES]
                                    * ubuf[lo:lo + rc + (SUBLANES if b else 0), c0:c0 + LANES])
                            zb = term if zb is None else zb + term
                    zs = pltpu.roll(zb, rc + SUBLANES - b, axis=0)[0:rc] if b else zb
                    y = zs if y is None else y + zs
                cbuf[r0:r0 + rc, c0:c0 + LANES] = y + cb_ref[:, c0:c0 + LANES]

    tn = ODD_TN
    for c in range(cw_width // tn):
        cols = slice(c * tn, (c + 1) * tn)
        a = jnp.dot(x, w_ref[:, c * tn:(c + 1) * tn], preferred_element_type=F32)
        gate = jnp.dot(x, w_ref[:, cw_width + c * tn:cw_width + (c + 1) * tn], preferred_element_type=F32)
        ubuf[CONV_HALO:CONV_HALO + tm, cols] = a * jax.nn.sigmoid(gate)
        if c > 0:
            conv_columns((c - 1) * tn, c * tn)
    for c in range(dw_width // PROJ_TN):
        cols = slice(c * PROJ_TN, (c + 1) * PROJ_TN)
        lo = 2 * cw_width + c * PROJ_TN
        zbuf[POOL_HALO:POOL_HALO + tm, cols] = jnp.dot(x, w_ref[:, lo:lo + PROJ_TN],
                                                       preferred_element_type=F32)
        if c == 0:
            conv_columns(cw_width - tn, cw_width)

    cv = cbuf[...]
    mu = jnp.mean(cv, axis=-1, keepdims=True)
    cen = cv - mu
    var = jnp.mean(cen * cen, axis=-1, keepdims=True)
    yn = cen * lax.rsqrt(var + EPS) * lg_ref[...] + lb_ref[...]
    u_ref[...] = (yn * jax.nn.sigmoid(yn)).astype(BF16)

    t_in_seq = (i % tiles_per_seq) * tm + lax.broadcasted_iota(jnp.int32, (tm, 1), 0)
    t1 = (t_in_seq + 1).astype(F32)
    gw = dw_width // len(POOL_WINDOWS)
    for g, kw in enumerate(POOL_WINDOWS):
        cols = slice(g * gw, (g + 1) * gw)
        zx = zbuf[0:POOL_HALO + tm, cols]
        tot = zx
        shift = 1
        while shift < kw:
            tot = tot + pltpu.roll(tot, shift, axis=0)
            shift *= 2
        z = zx[POOL_HALO:]
        pooled = tot[POOL_HALO:] * (1.0 / jnp.minimum(t1, float(kw))) - z
        y = jnp.dot(pooled.astype(BF16), pw_ref[g], preferred_element_type=F32)
        yd_ref[:, cols] = (y * ps_ref[:, cols]).astype(BF16)


def _odd_mixer(xn, w_in, conv_w, conv_b, ln_g, ln_b, pool_w, pool_scale, layer_idx, seq):
    n, d = xn.shape
    cw_width = conv_w.shape[-1]
    dw_width = pool_scale.shape[-1]
    taps = conv_w.shape[-2]
    assert taps - 1 <= CONV_HALO and max(POOL_WINDOWS) - 1 <= POOL_HALO
    tm = PROJ_TM
    ngroups, gw = pool_w.shape[1], pool_w.shape[2]
    vec = lambda width: pl.BlockSpec((None, 1, width), lambda i: (layer_idx, 0, 0))
    return pl.pallas_call(
        functools.partial(_odd_mixer_kernel, tiles_per_seq=seq // tm, conv_taps=taps),
        out_shape=[jax.ShapeDtypeStruct((n, cw_width), BF16),
                   jax.ShapeDtypeStruct((n, dw_width), BF16)],
        grid=(n // tm,),
        in_specs=[
            pl.BlockSpec((tm, d), lambda i: (i, 0)),
            _resident((None, d, 2 * cw_width + dw_width), lambda i: (layer_idx, 0, 0)),
            pl.BlockSpec((None, taps, cw_width), lambda i: (layer_idx, 0, 0)),
            vec(cw_width), vec(cw_width), vec(cw_width),
            pl.BlockSpec((None, ngroups, gw, gw), lambda i: (layer_idx, 0, 0, 0)),
            vec(dw_width),
        ],
        out_specs=[pl.BlockSpec((tm, cw_width), lambda i: (i, 0)),
                   pl.BlockSpec((tm, dw_width), lambda i: (i, 0))],
        scratch_shapes=[pltpu.VMEM((CONV_HALO + tm, cw_width), F32),
                        pltpu.VMEM((POOL_HALO + tm, dw_width), F32),
                        pltpu.VMEM((tm, cw_width), F32)],
        compiler_params=_params(("arbitrary",)),
        name="odd_mixer",
    )(xn, w_in, conv_w, conv_b, ln_g, ln_b, pool_w, pool_scale)


def _odd_out_kernel(x_ref, ya_ref, yb_ref, w_ref, out_ref):
    half = ya_ref.shape[-1]
    acc = jnp.dot(ya_ref[...], w_ref[0:half, :], preferred_element_type=F32)
    acc = acc + jnp.dot(yb_ref[...], w_ref[half:, :], preferred_element_type=F32)
    out_ref[...] = x_ref[...] + acc


def _odd_out(x, ya, yb, w_out, layer_idx):
    n, d = x.shape
    tm = PROJ_TM
    row = lambda i: (i, 0)
    return pl.pallas_call(
        _odd_out_kernel,
        out_shape=jax.ShapeDtypeStruct((n, d), F32),
        grid=(n // tm,),
        in_specs=[pl.BlockSpec((tm, d), row),
                  pl.BlockSpec((tm, ya.shape[-1]), row),
                  pl.BlockSpec((tm, yb.shape[-1]), row),
                  _resident((None, d, d), lambda i: (layer_idx, 0, 0))],
        out_specs=pl.BlockSpec((tm, d), row),
        compiler_params=_params(("parallel",)),
        name="odd_out",
    )(x, ya, yb, w_out)


def _by_plane(xn, batch, seq, planes):
    if planes == 1:
        return xn
    d = xn.shape[-1]
    x4 = xn.reshape(batch, seq // planes, planes, d)
    return jnp.swapaxes(x4, 1, 2).reshape(batch * seq, d)


def kernel(x, norm_g, ffn_w1, ffn_w3, ffn_w2, ev_w_in, ev_q_gain, ev_k_gain, ev_conv_w, ev_w_out,
           od_w_in, od_conv_w, od_conv_b, od_ln_g, od_ln_b, od_pool_w, od_pool_scale, od_w_out):
    batch, seq, d = x.shape
    depth = norm_g.shape[0]
    n = batch * seq
    f = ffn_w1.shape[-1]
    assert d - A_WIDTH == ev_conv_w.shape[-1] and seq % (DILATIONS[-1] * ATTN_BLOCK) == 0
    assert n % FFN_TM == 0 and seq % PROJ_TM == 0 and f % FFN_TF == 0 and f // FFN_TF >= 2
    assert all(hd % LSE_HEADS == 0 and N_HEADS % hd == 0 for hd in ATTN_HEADS)
    assert all((seq // p) % tl == 0 and tl % (ATTN_BLOCK * dl // p) == 0
               for dl, p, tl in zip(DILATIONS, ATTN_PLANES, ATTN_TL))

    w1, w3, w2 = (w.astype(BF16) for w in (ffn_w1, ffn_w3, ffn_w2))
    ev_in, ev_out, od_in, od_out = (w.astype(BF16) for w in (ev_w_in, ev_w_out, od_w_in, od_w_out))
    pool_w = od_pool_w.astype(BF16)
    gains = norm_g[:, :, None, :]
    row = lambda v: v[:, None, :]
    q_gain, k_gain = row(ev_q_gain), row(ev_k_gain)
    conv_b, ln_g, ln_b, pool_scale = row(od_conv_b), row(od_ln_g), row(od_ln_b), row(od_pool_scale)
    slopes = jnp.asarray(2.0 ** (-8.0 * np.arange(1, N_HEADS + 1) / N_HEADS), dtype=F32)

    h = x.reshape(n, d)
    for layer in range(depth):
        idx = layer // 2
        h, xn = _ffn(h, gains, w1, w3, w2, layer, 0, 0, next_gain_idx=1)
        if layer % 2 == 0:
            outs, lses = [], []
            by_plane = {p: _by_plane(xn, batch, seq, p) for p in set(ATTN_PLANES)}
            for g in range(len(DILATIONS)):
                q, k, v = _group_proj(by_plane[ATTN_PLANES[g]], ev_in, q_gain, k_gain, idx, g)
                o, lse = _attention_group(slopes, q, k, v, batch, seq, g)
                outs.append(o)
                lses.append(lse)
            yb = _short_conv(xn, ev_in, ev_conv_w, idx, seq)
            h = _even_out(h, outs, lses, yb, ev_out, idx)
        else:
            u, yd = _odd_mixer(xn, od_in, od_conv_w, conv_b, ln_g, ln_b, pool_w, pool_scale, idx, seq)
            h = _odd_out(h, u, yd, od_out, idx)
        h = _ffn(h, gains, w1, w3, w2, layer, 1, 2)
    return h.reshape(batch, seq, d)
```

```python
import functools

import numpy as np
import jax
import jax.numpy as jnp
from jax import lax
from jax.experimental import pallas as pl
from jax.experimental.pallas import tpu as pltpu

F32 = jnp.float32
BF16 = jnp.bfloat16
EPS = 1e-6

HEAD_DIM = 128
N_HEADS = 8
A_WIDTH = N_HEADS * HEAD_DIM
WINDOWS = (128, 512, 2048)
DILATIONS = (1, 4, 16)
ATTN_BLOCK = 128
POOL_WINDOWS = (2, 4, 8, 16)

LANES = 128
SUBLANES = 8
VMEM_LIMIT = 60000 * 1024
FFN_TM = 1024
FFN_TF = 512
PROJ_TM = 512
PROJ_TN = 512
ODD_TN = 256
ATTN_PLANES = (1, 16, 16)
ATTN_TL = (512, 64, 128)
ATTN_HEADS = (8, 8, 4)
LSE_HEADS = 4
CONV_HALO = 32
CONV_ROWS = 128
POOL_HALO = 16


def _params(semantics):
    return pltpu.CompilerParams(dimension_semantics=semantics, vmem_limit_bytes=VMEM_LIMIT)


def _resident(block_shape, index_map):
    return pl.BlockSpec(block_shape, index_map, pipeline_mode=pl.Buffered(1))


def _rms(x, gain):
    ms = jnp.mean(x * x, axis=-1, keepdims=True)
    return x * lax.rsqrt(ms + EPS) * gain


def _ffn_kernel(*refs, with_next):
    if with_next:
        x_hbm, g_ref, w1_ref, w3_ref, w2_ref, gn_ref, o_ref, xn_ref, xs_ref, xbuf, xsem = refs
    else:
        x_hbm, g_ref, w1_ref, w3_ref, w2_ref, o_ref, xs_ref, xbuf, xsem = refs
    i, j = pl.program_id(0), pl.program_id(1)
    tm = xbuf.shape[0]

    def x_copy(tile):
        return pltpu.make_async_copy(x_hbm.at[pl.ds(tile * tm, tm), :], xbuf, xsem)

    @pl.when((i == 0) & (j == 0))
    def _():
        x_copy(0).start()

    @pl.when(j == 0)
    def _():
        x_copy(i).wait()
        xf = xbuf[...]
        xs_ref[...] = _rms(xf, g_ref[...]).astype(BF16)
        o_ref[...] = xf

    @pl.when((j == 1) & (i + 1 < pl.num_programs(0)))
    def _():
        x_copy(i + 1).start()

    xn = xs_ref[...]
    h1 = jnp.dot(xn, w1_ref[...], preferred_element_type=F32)
    h3 = jnp.dot(xn, w3_ref[...], preferred_element_type=F32)
    a = (0.5 * (h1 * jax.nn.sigmoid(h1)) * h3).astype(BF16)
    o_ref[...] += jnp.dot(a, w2_ref[...], preferred_element_type=F32)

    if with_next:
        @pl.when(j == pl.num_programs(1) - 1)
        def _():
            xn_ref[...] = _rms(o_ref[...], gn_ref[...]).astype(BF16)


def _ffn(x, norm_g, w1, w3, w2, layer, which, gain_idx, next_gain_idx=None):
    n, d = x.shape
    tm, tf = FFN_TM, FFN_TF
    nf = w1.shape[-1] // tf
    with_next = next_gain_idx is not None
    in_specs = [
        pl.BlockSpec(memory_space=pl.ANY),
        pl.BlockSpec((None, None, 1, d), lambda i, j: (layer, gain_idx, 0, 0)),
        pl.BlockSpec((None, None, d, tf), lambda i, j: (layer, which, 0, j)),
        pl.BlockSpec((None, None, d, tf), lambda i, j: (layer, which, 0, j)),
        pl.BlockSpec((None, None, tf, d), lambda i, j: (layer, which, j, 0)),
    ]
    args = [x, norm_g, w1, w3, w2]
    out_shape = [jax.ShapeDtypeStruct((n, d), F32)]
    out_specs = [pl.BlockSpec((tm, d), lambda i, j: (i, 0))]
    if with_next:
        in_specs.append(pl.BlockSpec((None, None, 1, d), lambda i, j: (layer, next_gain_idx, 0, 0)))
        args.append(norm_g)
        out_shape.append(jax.ShapeDtypeStruct((n, d), BF16))
        out_specs.append(pl.BlockSpec((tm, d), lambda i, j: (i, 0)))
    res = pl.pallas_call(
        functools.partial(_ffn_kernel, with_next=with_next),
        out_shape=out_shape,
        grid=(n // tm, nf),
        in_specs=in_specs,
        out_specs=out_specs,
        scratch_shapes=[pltpu.VMEM((tm, d), BF16), pltpu.VMEM((tm, d), F32), pltpu.SemaphoreType.DMA(())],
        compiler_params=_params(("arbitrary", "arbitrary")),
        name="ffn_next" if with_next else "ffn",
    )(*args)
    return res if with_next else res[0]


def _group_proj_kernel(x_ref, wq_ref, wk_ref, wv_ref, qg_ref, kg_ref, q_out, k_out, v_out):
    x = x_ref[...]
    for w_ref, gain_ref, o_ref in ((wq_ref, qg_ref, q_out), (wk_ref, kg_ref, k_out), (wv_ref, None, v_out)):
        for c in range(A_WIDTH // PROJ_TN):
            h = jnp.dot(x, w_ref[:, c * PROJ_TN:(c + 1) * PROJ_TN], preferred_element_type=F32)
            for hh in range(PROJ_TN // HEAD_DIM):
                lo = c * PROJ_TN + hh * HEAD_DIM
                hs = h[:, hh * HEAD_DIM:(hh + 1) * HEAD_DIM]
                if gain_ref is not None:
                    hs = _rms(hs, gain_ref[...])
                o_ref[:, lo:lo + HEAD_DIM] = hs.astype(BF16)


def _group_proj(xg, w_in, q_gain, k_gain, layer_idx, group):
    n, d = xg.shape
    tm = PROJ_TM
    ngrp = len(DILATIONS)
    slab = lambda which: _resident((None, d, A_WIDTH), lambda i: (layer_idx, 0, which * ngrp + group))
    gain = pl.BlockSpec((None, 1, HEAD_DIM), lambda i: (layer_idx, 0, 0))
    out = pl.BlockSpec((tm, A_WIDTH), lambda i: (i, 0))
    return pl.pallas_call(
        _group_proj_kernel,
        out_shape=[jax.ShapeDtypeStruct((n, A_WIDTH), BF16)] * 3,
        grid=(n // tm,),
        in_specs=[pl.BlockSpec((tm, d), lambda i: (i, 0)), slab(0), slab(1), slab(2), gain, gain],
        out_specs=[out, out, out],
        compiler_params=_params(("parallel",)),
        name="group_proj",
    )(xg, w_in, w_in, w_in, q_gain, k_gain)


def _carry_history(buf, halo, tm, first):
    @pl.when(first)
    def _():
        buf[0:halo, :] = jnp.zeros((halo, buf.shape[-1]), buf.dtype)

    @pl.when(jnp.logical_not(first))
    def _():
        buf[0:halo, :] = buf[tm:tm + halo, :]


def _short_conv_kernel(x_ref, w_ref, cw_ref, o_ref, work_ref, *, tiles_per_seq):
    tm = x_ref.shape[0]
    width = o_ref.shape[-1]
    _carry_history(work_ref, SUBLANES, tm, (pl.program_id(0) % tiles_per_seq) == 0)
    x = x_ref[...]
    tn = ODD_TN

    def dots(c):
        def proj(k):
            lo = k * width + c * tn
            return jnp.dot(x, w_ref[:, lo:lo + tn], preferred_element_type=F32)
        return proj(0), proj(1), proj(2)

    def epilogue(c, bg, cg, xt):
        cols = slice(c * tn, (c + 1) * tn)
        u = cg * xt
        work_ref[SUBLANES:SUBLANES + tm, cols] = u
        y = (cw_ref[0:1, cols] * work_ref[SUBLANES - 2:SUBLANES - 2 + tm, cols]
             + cw_ref[1:2, cols] * work_ref[SUBLANES - 1:SUBLANES - 1 + tm, cols]
             + cw_ref[2:3, cols] * u)
        o_ref[:, cols] = (bg * y).astype(BF16)

    pending = None
    for c in range(width // tn):
        cur = dots(c)
        if pending is not None:
            epilogue(c - 1, *pending)
        pending = cur
    epilogue(width // tn - 1, *pending)


def _short_conv(xn, w_in, conv_w, layer_idx, seq):
    n, d = xn.shape
    width = d - A_WIDTH
    tm = PROJ_TM
    return pl.pallas_call(
        functools.partial(_short_conv_kernel, tiles_per_seq=seq // tm),
        out_shape=jax.ShapeDtypeStruct((n, width), BF16),
        grid=(n // tm,),
        in_specs=[
            pl.BlockSpec((tm, d), lambda i: (i, 0)),
            _resident((None, d, 3 * width), lambda i: (layer_idx, 0, 3)),
            pl.BlockSpec((None, 3, width), lambda i: (layer_idx, 0, 0)),
        ],
        out_specs=pl.BlockSpec((tm, width), lambda i: (i, 0)),
        scratch_shapes=[pltpu.VMEM((SUBLANES + tm, width), F32)],
        compiler_params=_params(("arbitrary",)),
        name="short_conv",
    )(xn, w_in, conv_w)


def _attn_kernel(slopes_ref, q_ref, kc_ref, kp_ref, vc_ref, vp_ref, o_ref, lse_ref, *, dilation):
    planes, tlp, hw = q_ref.shape
    blk = ATTN_BLOCK
    nchunks = planes // dilation
    chunk = blk // nchunks
    heads = hw // HEAD_DIM
    lt = pl.program_id(1)
    head0 = pl.program_id(2) * heads

    def pos(idx):
        return (idx % chunk) * nchunks + idx // chunk

    qi = lax.broadcasted_iota(jnp.int32, (blk, 2 * blk), 0)
    ci = lax.broadcasted_iota(jnp.int32, (blk, 2 * blk), 1)
    dist = pos(qi) - pos(ci % blk) + jnp.where(ci < blk, blk, 0)
    valid = (dist >= 0) & (dist <= blk)
    neg_inf = F32(-jnp.inf)
    neg_dist = jnp.where(valid, -(dist * dilation).astype(F32), neg_inf)
    has_prev = (lt > 0) | (ci >= blk)
    lane = lax.broadcasted_iota(jnp.int32, (blk, LANES), 1)
    scale = HEAD_DIM ** -0.5

    def gather(cur_ref, prev_ref, members, i, hc):
        now = [cur_ref[p, i * chunk:(i + 1) * chunk, hc] for p in members]
        if i == 0:
            before = [prev_ref[p, :, hc] for p in members]
        else:
            before = [cur_ref[p, (i - 1) * chunk:i * chunk, hc] for p in members]
        return jnp.concatenate(before + now, axis=0)

    for r in range(dilation):
        members = [r + dilation * c for c in range(nchunks)]
        for i in range(tlp // chunk):
            if planes == 1:
                dests = [pl.ds(i * chunk, chunk)]
            else:
                dests = [pl.ds(i * chunk * planes + p, chunk, stride=planes) for p in members]
            for h in range(heads):
                if h % LSE_HEADS == 0:
                    lse_tile = jnp.zeros((blk, LANES), F32)
                hc = slice(h * HEAD_DIM, (h + 1) * HEAD_DIM)
                q = jnp.concatenate([q_ref[p, i * chunk:(i + 1) * chunk, hc] for p in members], axis=0)
                kk = gather(kc_ref, kp_ref, members, i, hc)
                vv = gather(vc_ref, vp_ref, members, i, hc)
                s = lax.dot_general(q, kk, (((1,), (1,)), ((), ())), preferred_element_type=F32)
                s = s * scale + slopes_ref[head0 + h] * neg_dist
                if i == 0:
                    s = jnp.where(has_prev, s, neg_inf)
                m = jnp.max(s, axis=-1, keepdims=True)
                p = jnp.exp(s - m)
                l = jnp.sum(p, axis=-1, keepdims=True)
                o = jnp.dot(p.astype(BF16), vv, preferred_element_type=F32) * (1.0 / l)
                lse_tile = jnp.where(lane == h % LSE_HEADS, m + jnp.log(l), lse_tile)
                for c, dest in enumerate(dests):
                    o_ref[h, dest, :] = o[c * chunk:(c + 1) * chunk]
                    if h % LSE_HEADS == LSE_HEADS - 1:
                        lse_ref[h // LSE_HEADS, dest, :] = lse_tile[c * chunk:(c + 1) * chunk]


def _attention_group(slopes, q, k, v, batch, seq, group):
    d = DILATIONS[group]
    planes = ATTN_PLANES[group]
    assert WINDOWS[group] // d == ATTN_BLOCK and planes % d == 0
    rows = seq // planes
    tl = ATTN_TL[group]
    chunk = ATTN_BLOCK * d // planes
    heads = ATTN_HEADS[group]
    hw = heads * HEAD_DIM
    span = tl * planes
    shape = (batch, planes, rows, A_WIDTH)
    qv, kv, vv = q.reshape(shape), k.reshape(shape), v.reshape(shape)
    cur = pl.BlockSpec((None, planes, tl, hw), lambda b, lt, hx: (b, 0, lt, hx))
    prev = pl.BlockSpec((None, planes, chunk, hw),
                        lambda b, lt, hx: (b, 0, jnp.maximum(lt * (tl // chunk) - 1, 0), hx))
    o, lse = pl.pallas_call(
        functools.partial(_attn_kernel, dilation=d),
        out_shape=[jax.ShapeDtypeStruct((N_HEADS, batch, seq, HEAD_DIM), F32),
                   jax.ShapeDtypeStruct((N_HEADS // LSE_HEADS, batch, seq, LANES), F32)],
        grid=(batch, rows // tl, N_HEADS // heads),
        in_specs=[pl.BlockSpec(memory_space=pltpu.SMEM), cur, cur, prev, cur, prev],
        out_specs=[pl.BlockSpec((heads, None, span, HEAD_DIM), lambda b, lt, hx: (hx, b, lt, 0)),
                   pl.BlockSpec((heads // LSE_HEADS, None, span, LANES), lambda b, lt, hx: (hx, b, lt, 0))],
        compiler_params=_params(("parallel", "parallel", "parallel")),
        name=f"attn_d{d}",
    )(slopes, qv, kv, kv, vv, vv)
    n = batch * seq
    return o.reshape(N_HEADS, n, HEAD_DIM), lse.reshape(N_HEADS // LSE_HEADS, n, LANES)


def _even_out_kernel(x_ref, o0_ref, o1_ref, o2_ref, l0_ref, l1_ref, l2_ref, yb_ref, w_ref, out_ref):
    lses = [l0_ref[...], l1_ref[...], l2_ref[...]]
    mx = jnp.maximum(jnp.maximum(lses[0], lses[1]), lses[2])
    es = [jnp.exp(l - mx) for l in lses]
    inv = 1.0 / (es[0] + es[1] + es[2])
    alphas = [e * inv for e in es]
    o_refs = (o0_ref, o1_ref, o2_ref)
    parts = []
    for h in range(N_HEADS):
        chunk, lane = divmod(h, LSE_HEADS)
        y = alphas[0][chunk][:, lane:lane + 1] * o_refs[0][h]
        for g in (1, 2):
            y = y + alphas[g][chunk][:, lane:lane + 1] * o_refs[g][h]
        parts.append(y.astype(BF16))
    ya = jnp.concatenate(parts, axis=-1)
    acc = jnp.dot(ya, w_ref[0:A_WIDTH, :], preferred_element_type=F32)
    acc = acc + jnp.dot(yb_ref[...], w_ref[A_WIDTH:, :], preferred_element_type=F32)
    out_ref[...] = x_ref[...] + acc


def _even_out(x, outs, lses, yb, w_out, layer_idx):
    n, d = x.shape
    tm = PROJ_TM
    nchunk = N_HEADS // LSE_HEADS
    row = lambda i: (i, 0)
    return pl.pallas_call(
        _even_out_kernel,
        out_shape=jax.ShapeDtypeStruct((n, d), F32),
        grid=(n // tm,),
        in_specs=[pl.BlockSpec((tm, d), row)]
        + [pl.BlockSpec((N_HEADS, tm, HEAD_DIM), lambda i: (0, i, 0))] * 3
        + [pl.BlockSpec((nchunk, tm, LANES), lambda i: (0, i, 0))] * 3
        + [pl.BlockSpec((tm, d - A_WIDTH), row),
           _resident((None, d, d), lambda i: (layer_idx, 0, 0))],
        out_specs=pl.BlockSpec((tm, d), row),
        compiler_params=_params(("parallel",)),
        name="even_out",
    )(x, *outs, *lses, yb, w_out)


def _odd_mixer_kernel(x_ref, w_ref, cw_ref, cb_ref, lg_ref, lb_ref, pw_ref, ps_ref,
                      u_ref, yd_ref, ubuf, zbuf, cbuf, *, tiles_per_seq, conv_taps):
    tm = x_ref.shape[0]
    cw_width = u_ref.shape[-1]
    dw_width = yd_ref.shape[-1]
    i = pl.program_id(0)
    first = (i % tiles_per_seq) == 0
    x = x_ref[...]
    _carry_history(ubuf, CONV_HALO, tm, first)
    _carry_history(zbuf, POOL_HALO, tm, first)

    lead = CONV_HALO - (conv_taps - 1)
    rc = CONV_ROWS

    def conv_columns(lo_col, hi_col):
        for r0 in range(0, tm, rc):
            for c0 in range(lo_col, hi_col, LANES):
                y = None
                for b in range(SUBLANES):
                    zb = None
                    for a in range(-(-(conv_taps + lead) // SUBLANES)):
                        j = SUBLANES * a + b - lead
                        if 0 <= j < conv_taps:
                            lo = r0 + SUBLANES * a
                            term = (cw_ref[j:j + 1, c0:c0 + LANES]
                                    * ubuf[lo:lo + rc + (SUBLANES if b else 0), c0:c0 + LANES])
                            zb = term if zb is None else zb + term
                    zs = pltpu.roll(zb, rc + SUBLANES - b, axis=0)[0:rc] if b else zb
                    y = zs if y is None else y + zs
                cbuf[r0:r0 + rc, c0:c0 + LANES] = y + cb_ref[:, c0:c0 + LANES]

    tn = ODD_TN
    for c in range(cw_width // tn):
        cols = slice(c * tn, (c + 1) * tn)
        a = jnp.dot(x, w_ref[:, c * tn:(c + 1) * tn], preferred_element_type=F32)
        gate = jnp.dot(x, w_ref[:, cw_width + c * tn:cw_width + (c + 1) * tn], preferred_element_type=F32)
        ubuf[CONV_HALO:CONV_HALO + tm, cols] = a * jax.nn.sigmoid(gate)
        if c > 0:
            conv_columns((c - 1) * tn, c * tn)
    for c in range(dw_width // PROJ_TN):
        cols = slice(c * PROJ_TN, (c + 1) * PROJ_TN)
        lo = 2 * cw_width + c * PROJ_TN
        zbuf[POOL_HALO:POOL_HALO + tm, cols] = jnp.dot(x, w_ref[:, lo:lo + PROJ_TN],
                                                       preferred_element_type=F32)
        if c == 0:
            conv_columns(cw_width - tn, cw_width)

    cv = cbuf[...]
    mu = jnp.mean(cv, axis=-1, keepdims=True)
    cen = cv - mu
    var = jnp.mean(cen * cen, axis=-1, keepdims=True)
    yn = cen * lax.rsqrt(var + EPS) * lg_ref[...] + lb_ref[...]
    u_ref[...] = (yn * jax.nn.sigmoid(yn)).astype(BF16)

    t_in_seq = (i % tiles_per_seq) * tm + lax.broadcasted_iota(jnp.int32, (tm, 1), 0)
    t1 = (t_in_seq + 1).astype(F32)
    gw = dw_width // len(POOL_WINDOWS)
    for g, kw in enumerate(POOL_WINDOWS):
        cols = slice(g * gw, (g + 1) * gw)
        zx = zbuf[0:POOL_HALO + tm, cols]
        tot = zx
        shift = 1
        while shift < kw:
            tot = tot + pltpu.roll(tot, shift, axis=0)
            shift *= 2
        z = zx[POOL_HALO:]
        pooled = tot[POOL_HALO:] * (1.0 / jnp.minimum(t1, float(kw))) - z
        y = jnp.dot(pooled.astype(BF16), pw_ref[g], preferred_element_type=F32)
        yd_ref[:, cols] = (y * ps_ref[:, cols]).astype(BF16)


def _odd_mixer(xn, w_in, conv_w, conv_b, ln_g, ln_b, pool_w, pool_scale, layer_idx, seq):
    n, d = xn.shape
    cw_width = conv_w.shape[-1]
    dw_width = pool_scale.shape[-1]
    taps = conv_w.shape[-2]
    assert taps - 1 <= CONV_HALO and max(POOL_WINDOWS) - 1 <= POOL_HALO
    tm = PROJ_TM
    ngroups, gw = pool_w.shape[1], pool_w.shape[2]
    vec = lambda width: pl.BlockSpec((None, 1, width), lambda i: (layer_idx, 0, 0))
    return pl.pallas_call(
        functools.partial(_odd_mixer_kernel, tiles_per_seq=seq // tm, conv_taps=taps),
        out_shape=[jax.ShapeDtypeStruct((n, cw_width), BF16),
                   jax.ShapeDtypeStruct((n, dw_width), BF16)],
        grid=(n // tm,),
        in_specs=[
            pl.BlockSpec((tm, d), lambda i: (i, 0)),
            _resident((None, d, 2 * cw_width + dw_width), lambda i: (layer_idx, 0, 0)),
            pl.BlockSpec((None, taps, cw_width), lambda i: (layer_idx, 0, 0)),
            vec(cw_width), vec(cw_width), vec(cw_width),
            pl.BlockSpec((None, ngroups, gw, gw), lambda i: (layer_idx, 0, 0, 0)),
            vec(dw_width),
        ],
        out_specs=[pl.BlockSpec((tm, cw_width), lambda i: (i, 0)),
                   pl.BlockSpec((tm, dw_width), lambda i: (i, 0))],
        scratch_shapes=[pltpu.VMEM((CONV_HALO + tm, cw_width), F32),
                        pltpu.VMEM((POOL_HALO + tm, dw_width), F32),
                        pltpu.VMEM((tm, cw_width), F32)],
        compiler_params=_params(("arbitrary",)),
        name="odd_mixer",
    )(xn, w_in, conv_w, conv_b, ln_g, ln_b, pool_w, pool_scale)


def _odd_out_kernel(x_ref, ya_ref, yb_ref, w_ref, out_ref):
    half = ya_ref.shape[-1]
    acc = jnp.dot(ya_ref[...], w_ref[0:half, :], preferred_element_type=F32)
    acc = acc + jnp.dot(yb_ref[...], w_ref[half:, :], preferred_element_type=F32)
    out_ref[...] = x_ref[...] + acc


def _odd_out(x, ya, yb, w_out, layer_idx):
    n, d = x.shape
    tm = PROJ_TM
    row = lambda i: (i, 0)
    return pl.pallas_call(
        _odd_out_kernel,
        out_shape=jax.ShapeDtypeStruct((n, d), F32),
        grid=(n // tm,),
        in_specs=[pl.BlockSpec((tm, d), row),
                  pl.BlockSpec((tm, ya.shape[-1]), row),
                  pl.BlockSpec((tm, yb.shape[-1]), row),
                  _resident((None, d, d), lambda i: (layer_idx, 0, 0))],
        out_specs=pl.BlockSpec((tm, d), row),
        compiler_params=_params(("parallel",)),
        name="odd_out",
    )(x, ya, yb, w_out)


def _by_plane(xn, batch, seq, planes):
    if planes == 1:
        return xn
    d = xn.shape[-1]
    x4 = xn.reshape(batch, seq // planes, planes, d)
    return jnp.swapaxes(x4, 1, 2).reshape(batch * seq, d)


def kernel(x, norm_g, ffn_w1, ffn_w3, ffn_w2, ev_w_in, ev_q_gain, ev_k_gain, ev_conv_w, ev_w_out,
           od_w_in, od_conv_w, od_conv_b, od_ln_g, od_ln_b, od_pool_w, od_pool_scale, od_w_out):
    batch, seq, d = x.shape
    depth = norm_g.shape[0]
    n = batch * seq
    f = ffn_w1.shape[-1]
    assert d - A_WIDTH == ev_conv_w.shape[-1] and seq % (DILATIONS[-1] * ATTN_BLOCK) == 0
    assert n % FFN_TM == 0 and seq % PROJ_TM == 0 and f % FFN_TF == 0 and f // FFN_TF >= 2
    assert all(hd % LSE_HEADS == 0 and N_HEADS % hd == 0 for hd in ATTN_HEADS)
    assert all((seq // p) % tl == 0 and tl % (ATTN_BLOCK * dl // p) == 0
               for dl, p, tl in zip(DILATIONS, ATTN_PLANES, ATTN_TL))

    w1, w3, w2 = (w.astype(BF16) for w in (ffn_w1, ffn_w3, ffn_w2))
    ev_in, ev_out, od_in, od_out = (w.astype(BF16) for w in (ev_w_in, ev_w_out, od_w_in, od_w_out))
    pool_w = od_pool_w.astype(BF16)
    gains = norm_g[:, :, None, :]
    row = lambda v: v[:, None, :]
    q_gain, k_gain = row(ev_q_gain), row(ev_k_gain)
    conv_b, ln_g, ln_b, pool_scale = row(od_conv_b), row(od_ln_g), row(od_ln_b), row(od_pool_scale)
    slopes = jnp.asarray(2.0 ** (-8.0 * np.arange(1, N_HEADS + 1) / N_HEADS), dtype=F32)

    h = x.reshape(n, d)
    for layer in range(depth):
        idx = layer // 2
        h, xn = _ffn(h, gains, w1, w3, w2, layer, 0, 0, next_gain_idx=1)
        if layer % 2 == 0:
            outs, lses = [], []
            by_plane = {p: _by_plane(xn, batch, seq, p) for p in set(ATTN_PLANES)}
            for g in range(len(DILATIONS)):
                q, k, v = _group_proj(by_plane[ATTN_PLANES[g]], ev_in, q_gain, k_gain, idx, g)
                o, lse = _attention_group(slopes, q, k, v, batch, seq, g)
                outs.append(o)
                lses.append(lse)
            yb = _short_conv(xn, ev_in, ev_conv_w, idx, seq)
            h = _even_out(h, outs, lses, yb, ev_out, idx)
        else:
            u, yd = _odd_mixer(xn, od_in, od_conv_w, conv_b, ln_g, ln_b, pool_w, pool_scale, idx, seq)
            h = _odd_out(h, u, yd, od_out, idx)
        h = _ffn(h, gains, w1, w3, w2, layer, 1, 2)
    return h.reshape(batch, seq, d)
```
